```python
import jax, jax.numpy as jnp
from jax import lax
import numpy as np

D_MODEL = 1024
BATCH = 4
SEQ = 8192
DEPTH = 2

SSD_D_INNER = 1024
SSD_HEAD_DIM = 64
SSD_HEADS = SSD_D_INNER // SSD_HEAD_DIM
SSD_GROUPS = 4
SSD_STATE = 128
SSD_CONV = 4
SSD_CHUNK = 128
SSD_CONV_CH = SSD_D_INNER + 2 * SSD_GROUPS * SSD_STATE
S5_WIDTH = 1024
S5_GROUP = 16
S5_GROUPS = S5_WIDTH // S5_GROUP
S5_STATE = 64
MLA_HEADS = 16
MLA_NOPE = 64
MLA_ROPE = 32
MLA_V = 64
MLA_Q_RANK = 512
MLA_KV_RANK = 256
MLA_WIDTH = MLA_HEADS * MLA_V
ROPE_THETA = 10000.0
Q_BLOCK = 128
N_BRANCH = 3
EPS = 1e-6
NEG_INF = -1e30

IN_SIZES = (SSD_D_INNER, SSD_CONV_CH, SSD_HEADS, S5_WIDTH, S5_WIDTH,
            MLA_Q_RANK, MLA_KV_RANK, MLA_ROPE, MLA_WIDTH, N_BRANCH * D_MODEL)
D_IN = (SSD_D_INNER + SSD_CONV_CH + SSD_HEADS + S5_WIDTH + S5_WIDTH
        + MLA_Q_RANK + MLA_KV_RANK + MLA_ROPE + MLA_WIDTH + N_BRANCH * D_MODEL)

kernel_name = "hybrid_ssd_s5_mla_gated_parallel"


def rms_norm(x, g):
    xf = x.astype(jnp.float32)
    y = xf * lax.rsqrt(jnp.mean(xf * xf, axis=-1, keepdims=True) + EPS)
    return (y * g.astype(jnp.float32)).astype(x.dtype)


def split_columns(p):
    outs = []
    start = 0
    for n in IN_SIZES:
        outs.append(p[..., start:start + n])
        start += n
    return outs


def causal_dwconv(x, w, b):
    k = w.shape[0]
    y = lax.conv_general_dilated(x, w[:, None, :].astype(x.dtype), window_strides=(1,),
                                 padding=[(k - 1, 0)], dimension_numbers=('NWC', 'WIO', 'NWC'),
                                 feature_group_count=x.shape[-1])
    return y + b


def segsum(a):
    cs = jnp.cumsum(a, axis=-1)
    d = cs[..., :, None] - cs[..., None, :]
    n = a.shape[-1]
    mask = jnp.tril(jnp.ones((n, n), dtype=bool))
    return jnp.where(mask, d, -jnp.inf)


def ssd_chunked(x, dt, a, bmat, cmat):
    bsz, s, h, p = x.shape
    g, n = bmat.shape[-2:]
    r = h // g
    l = SSD_CHUNK
    nc = s // l
    xd = (x * dt[..., None]).reshape(bsz, nc, l, g, r, p)
    adt = (dt * a).reshape(bsz, nc, l, g, r).transpose(0, 1, 3, 4, 2)
    bc = bmat.reshape(bsz, nc, l, g, n)
    cc = cmat.reshape(bsz, nc, l, g, n)
    a_cs = jnp.cumsum(adt, axis=-1)
    cb = jnp.einsum('bclgn,bcsgn->bcgls', cc, bc)
    m = cb[:, :, :, None] * jnp.exp(segsum(adt))
    y_diag = jnp.einsum('bcgrls,bcsgrp->bclgrp', m, xd)
    decay_to_end = jnp.exp(a_cs[..., -1:] - a_cs).transpose(0, 1, 4, 2, 3)
    chunk_states = jnp.einsum('bclgn,bclgrp->cbgrpn', bc, xd * decay_to_end[..., None])
    chunk_decay = jnp.exp(a_cs[..., -1]).transpose(1, 0, 2, 3)

    def step(state, inp):
        st, dec = inp
        return state * dec[..., None, None] + st, state

    h0 = jnp.zeros((bsz, g, r, p, n), jnp.float32)
    _, prev = lax.scan(step, h0, (chunk_states, chunk_decay))
    decay_in = jnp.exp(a_cs).transpose(0, 1, 4, 2, 3)[..., None]
    y_off = jnp.einsum('bclgn,cbgrpn->bclgrp', cc, prev) * decay_in
    return (y_diag + y_off).reshape(bsz, s, h, p)


def ssd_branch(z, xbc, dt_raw, conv_w, conv_b, dt_bias, a_log, d_skip, norm_g):
    bsz, s, _ = z.shape
    xbc = jax.nn.silu(causal_dwconv(xbc, conv_w, conv_b)).astype(jnp.float32)
    xs = xbc[..., :SSD_D_INNER].reshape(bsz, s, SSD_HEADS, SSD_HEAD_DIM)
    bm = xbc[..., SSD_D_INNER:SSD_D_INNER + SSD_GROUPS * SSD_STATE].reshape(bsz, s, SSD_GROUPS, SSD_STATE)
    cm = xbc[..., SSD_D_INNER + SSD_GROUPS * SSD_STATE:].reshape(bsz, s, SSD_GROUPS, SSD_STATE)
    dt = jax.nn.softplus(dt_raw.astype(jnp.float32) + dt_bias.astype(jnp.float32))
    a = -jnp.exp(a_log.astype(jnp.float32))
    y = ssd_chunked(xs, dt, a, bm, cm) + d_skip.astype(jnp.float32)[:, None] * xs
    y = y.reshape(bsz, s, SSD_D_INNER) * jax.nn.silu(z.astype(jnp.float32))
    return rms_norm(y, norm_g).astype(z.dtype)


def complex_combine(e1, e2):
    a1r, a1i, b1r, b1i = e1
    a2r, a2i, b2r, b2i = e2
    return (a2r * a1r - a2i * a1i,
            a2r * a1i + a2i * a1r,
            a2r * b1r - a2i * b1i + b2r,
            a2r * b1i + a2i * b1r + b2i)


def s5_branch(u, z, log_step, lam_re, lam_im, b_re, b_im, c_re, c_im, d_skip, w_glu, b_glu):
    bsz, s, _ = u.shape
    f32 = jnp.float32
    uf = u.astype(f32).reshape(bsz, s, S5_GROUPS, S5_GROUP)
    step = jnp.exp(log_step.astype(f32))[:, None]
    lr, li = lam_re.astype(f32), lam_im.astype(f32)
    mag = jnp.exp(lr * step)
    ang = li * step
    abar_re, abar_im = mag * jnp.cos(ang), mag * jnp.sin(ang)
    den = lr * lr + li * li
    nr, ni = abar_re - 1.0, abar_im
    f_re = (nr * lr + ni * li) / den
    f_im = (ni * lr - nr * li) / den
    br, bi = b_re.astype(f32), b_im.astype(f32)
    bb_re = f_re[..., None] * br - f_im[..., None] * bi
    bb_im = f_re[..., None] * bi + f_im[..., None] * br
    bu_re = jnp.einsum('gph,bsgh->bsgp', bb_re, uf)
    bu_im = jnp.einsum('gph,bsgh->bsgp', bb_im, uf)
    a_re = jnp.broadcast_to(abar_re, (1, s, S5_GROUPS, S5_STATE))
    a_im = jnp.broadcast_to(abar_im, (1, s, S5_GROUPS, S5_STATE))
    _, _, xr, xi = lax.associative_scan(complex_combine, (a_re, a_im, bu_re, bu_im), axis=1)
    y = (jnp.einsum('ghp,bsgp->bsgh', c_re.astype(f32), xr)
         - jnp.einsum('ghp,bsgp->bsgh', c_im.astype(f32), xi))
    y = y + d_skip.astype(f32).reshape(S5_GROUPS, S5_GROUP) * uf
    y = y.reshape(bsz, s, S5_WIDTH)
    gl = jax.nn.gelu(y)
    y = gl * jax.nn.sigmoid(gl @ w_glu.astype(f32) + b_glu.astype(f32))
    y = y * jax.nn.silu(z.astype(f32))
    return y.astype(u.dtype)


def apply_rope(x, cos, sin):
    half = x.shape[-1] // 2
    x1, x2 = x[..., :half], x[..., half:]
    return jnp.concatenate([x1 * cos - x2 * sin, x1 * sin + x2 * cos], axis=-1)


def mla_branch(c_q, c_kv, k_rope, z, q_norm, w_uq, kv_norm, w_ukv, cos, sin):
    bsz, s, _ = c_q.shape
    q = (rms_norm(c_q, q_norm) @ w_uq).reshape(bsz, s, MLA_HEADS, MLA_NOPE + MLA_ROPE)
    q_nope = q[..., :MLA_NOPE]
    q_rope = apply_rope(q[..., MLA_NOPE:], cos[:, None, :], sin[:, None, :]).astype(q.dtype)
    kv = (rms_norm(c_kv, kv_norm) @ w_ukv).reshape(bsz, s, MLA_HEADS, MLA_NOPE + MLA_V)
    k_nope, v = kv[..., :MLA_NOPE], kv[..., MLA_NOPE:]
    k_r = apply_rope(k_rope, cos, sin).astype(q.dtype)
    scale = (MLA_NOPE + MLA_ROPE) ** -0.5
    nb = s // Q_BLOCK
    qn_b = q_nope.reshape(bsz, nb, Q_BLOCK, MLA_HEADS, MLA_NOPE).swapaxes(0, 1)
    qr_b = q_rope.reshape(bsz, nb, Q_BLOCK, MLA_HEADS, MLA_ROPE).swapaxes(0, 1)
    kpos = jnp.arange(s)

    def attend(blk):
        qn, qr, i = blk
        sc = (jnp.einsum('bqhd,bkhd->bhqk', qn, k_nope)
              + jnp.einsum('bqhr,bkr->bhqk', qr, k_r)).astype(jnp.float32) * scale
        qpos = i * Q_BLOCK + jnp.arange(Q_BLOCK)
        sc = jnp.where(kpos[None, :] <= qpos[:, None], sc, NEG_INF)
        pr = jax.nn.softmax(sc, axis=-1).astype(v.dtype)
        return jnp.einsum('bhqk,bkhd->bqhd', pr, v)

    o = lax.map(attend, (qn_b, qr_b, jnp.arange(nb)))
    o = o.swapaxes(0, 1).reshape(bsz, s, MLA_WIDTH)
    return (o * jax.nn.silu(z.astype(o.dtype))).astype(z.dtype)


def setup_inputs(seed: int = 0) -> dict:
    key = jax.random.key(seed)
    ks = jax.random.split(key, 32)
    f32 = jnp.float32

    def nrm(k, shape, scale):
        return jax.random.normal(k, shape, f32) * scale

    def gain(k, n):
        return 1.0 + 0.02 * jax.random.normal(k, (DEPTH, n), f32)

    dt0 = jnp.exp(jax.random.uniform(ks[4], (DEPTH, SSD_HEADS), f32, np.log(1e-3), np.log(1e-1)))
    lam_im = jnp.broadcast_to(jnp.pi * jnp.arange(S5_STATE, dtype=f32), (DEPTH, S5_GROUPS, S5_STATE))
    return {
        "x": jax.random.normal(ks[0], (BATCH, SEQ, D_MODEL), f32),
        "pre_norm": gain(ks[1], D_MODEL),
        "w_in": nrm(ks[2], (DEPTH, D_MODEL, D_IN), D_MODEL ** -0.5),
        "conv_w": nrm(ks[3], (DEPTH, SSD_CONV, SSD_CONV_CH), SSD_CONV ** -0.5),
        "conv_b": nrm(ks[5], (DEPTH, SSD_CONV_CH), 0.02),
        "dt_bias": dt0 + jnp.log(-jnp.expm1(-dt0)),
        "a_log": jnp.log(jax.random.uniform(ks[6], (DEPTH, SSD_HEADS), f32, 1.0, 16.0)),
        "d_ssd": gain(ks[7], SSD_HEADS),
        "ssd_norm": gain(ks[8], SSD_D_INNER),
        "w_a": nrm(ks[9], (DEPTH, SSD_D_INNER, D_MODEL), SSD_D_INNER ** -0.5),
        "s5_log_step": jax.random.uniform(ks[10], (DEPTH, S5_GROUPS), f32, np.log(1e-3), np.log(1e-1)),
        "s5_lambda_re": -0.5 + nrm(ks[11], (DEPTH, S5_GROUPS, S5_STATE), 0.01),
        "s5_lambda_im": lam_im,
        "s5_b_re": nrm(ks[12], (DEPTH, S5_GROUPS, S5_STATE, S5_GROUP), (2 * S5_GROUP) ** -0.5),
        "s5_b_im": nrm(ks[13], (DEPTH, S5_GROUPS, S5_STATE, S5_GROUP), (2 * S5_GROUP) ** -0.5),
        "s5_c_re": nrm(ks[14], (DEPTH, S5_GROUPS, S5_GROUP, S5_STATE), (2 * S5_STATE) ** -0.5),
        "s5_c_im": nrm(ks[15], (DEPTH, S5_GROUPS, S5_GROUP, S5_STATE), (2 * S5_STATE) ** -0.5),
        "s5_d": nrm(ks[16], (DEPTH, S5_WIDTH), 1.0),
        "w_glu": nrm(ks[17], (DEPTH, S5_WIDTH, S5_WIDTH), S5_WIDTH ** -0.5),
        "b_glu": nrm(ks[18], (DEPTH, S5_WIDTH), 0.02),
        "w_b": nrm(ks[19], (DEPTH, S5_WIDTH, D_MODEL), S5_WIDTH ** -0.5),
        "q_norm": gain(ks[20], MLA_Q_RANK),
        "w_uq": nrm(ks[21], (DEPTH, MLA_Q_RANK, MLA_HEADS * (MLA_NOPE + MLA_ROPE)), MLA_Q_RANK ** -0.5),
        "kv_norm": gain(ks[22], MLA_KV_RANK),
        "w_ukv": nrm(ks[23], (DEPTH, MLA_KV_RANK, MLA_HEADS * (MLA_NOPE + MLA_V)), MLA_KV_RANK ** -0.5),
        "w_c": nrm(ks[24], (DEPTH, MLA_WIDTH, D_MODEL), MLA_WIDTH ** -0.5),
        "w_o": nrm(ks[25], (DEPTH, D_MODEL, D_MODEL), D_MODEL ** -0.5),
        "post_norm": gain(ks[26], D_MODEL),
    }


def reference(x, pre_norm, w_in, conv_w, conv_b, dt_bias, a_log, d_ssd, ssd_norm, w_a,
              s5_log_step, s5_lambda_re, s5_lambda_im, s5_b_re, s5_b_im, s5_c_re, s5_c_im,
              s5_d, w_glu, b_glu, w_b, q_norm, w_uq, kv_norm, w_ukv, w_c, w_o, post_norm):
    bsz, s, _ = x.shape
    pos = jnp.arange(s, dtype=jnp.float32)
    inv_freq = ROPE_THETA ** (-jnp.arange(0, MLA_ROPE, 2, dtype=jnp.float32) / MLA_ROPE)
    ang = pos[:, None] * inv_freq[None, :]
    cos, sin = jnp.cos(ang), jnp.sin(ang)
    for l in range(DEPTH):
        h = rms_norm(x, pre_norm[l])
        proj = h @ w_in[l]
        z_a, xbc, dt_raw, u_b, z_b, c_q, c_kv, k_rope, z_c, gate_logits = split_columns(proj)
        y_a = ssd_branch(z_a, xbc, dt_raw, conv_w[l], conv_b[l], dt_bias[l], a_log[l],
                         d_ssd[l], ssd_norm[l]) @ w_a[l]
        y_b = s5_branch(u_b, z_b, s5_log_step[l], s5_lambda_re[l], s5_lambda_im[l],
                        s5_b_re[l], s5_b_im[l], s5_c_re[l], s5_c_im[l], s5_d[l],
                        w_glu[l], b_glu[l]) @ w_b[l]
        y_c = mla_branch(c_q, c_kv, k_rope, z_c, q_norm[l], w_uq[l], kv_norm[l], w_ukv[l],
                         cos, sin) @ w_c[l]
        gates = jax.nn.sigmoid(gate_logits.astype(jnp.float32)).reshape(bsz, s, N_BRANCH, D_MODEL)
        merged = gates[..., 0, :] * y_a + gates[..., 1, :] * y_b + gates[..., 2, :] * y_c
        out = merged.astype(x.dtype) @ w_o[l]
        x = x + rms_norm(out, post_norm[l]).astype(x.dtype)
    return x
```

```python
import functools

import jax
import jax.numpy as jnp
from jax import lax
from jax.experimental import pallas as pl
from jax.experimental.pallas import tpu as pltpu

F32 = jnp.float32
BF16 = jnp.bfloat16

D_MODEL = 1024
SSD_D_INNER = 1024
SSD_HEAD_DIM = 64
SSD_HEADS = 16
SSD_GROUPS = 4
SSD_STATE = 128
SSD_CONV = 4
SSD_CHUNK = 128
S5_WIDTH = 1024
S5_GROUP = 16
S5_GROUPS = 64
S5_STATE = 64
MLA_HEADS = 16
MLA_NOPE = 64
MLA_ROPE = 32
MLA_V = 64
MLA_Q_RANK = 512
MLA_KV_RANK = 256
MLA_WIDTH = 1024
ROPE_THETA = 10000.0
EPS = 1e-6
MASK_VALUE = -1e30

LANES = 128
S5_L = 32
S5_K = S5_L * S5_GROUP
HEAD_PAD = 128
ROPE_LO = MLA_NOPE
VMEM_LIMIT = 56 * 1024 * 1024


def _cparams(sem):
    return pltpu.CompilerParams(dimension_semantics=sem, vmem_limit_bytes=VMEM_LIMIT)


def _silu(x):
    return x * jax.nn.sigmoid(x)


def _rms(x, g):
    return x * lax.rsqrt(jnp.mean(x * x, axis=-1, keepdims=True) + EPS) * g


N_MAIN = 9
TAIL_CQ, TAIL_CKV, TAIL_SMALL = 0, 512, 768


def _inproj_kernel(x_ref, g_ref, w_ref, main_ref, tail_ref, h_scr):
    j = pl.program_id(1)

    @pl.when(j == 0)
    def _():
        h_scr[...] = _rms(x_ref[...], g_ref[...]).astype(BF16)

    acc = jnp.dot(h_scr[...], w_ref[...], preferred_element_type=F32)

    @pl.when(j < N_MAIN)
    def _():
        main_ref[...] = acc.astype(BF16)

    @pl.when(j == N_MAIN)
    def _():
        tail_ref[...] = acc


def _inproj(x2d, gain, w_cat, tm):
    t = x2d.shape[0]
    return pl.pallas_call(
        _inproj_kernel,
        grid=(t // tm, N_MAIN + 1),
        in_specs=[
            pl.BlockSpec((tm, D_MODEL), lambda i, j: (i, 0)),
            pl.BlockSpec((1, D_MODEL), lambda i, j: (0, 0)),
            pl.BlockSpec((D_MODEL, 1024), lambda i, j: (0, j)),
        ],
        out_specs=[
            pl.BlockSpec((tm, 1024), lambda i, j: (i, jnp.minimum(j, N_MAIN - 1))),
            pl.BlockSpec((tm, 1024), lambda i, j: (i, 0)),
        ],
        out_shape=[
            jax.ShapeDtypeStruct((t, N_MAIN * 1024), BF16),
            jax.ShapeDtypeStruct((t, 1024), F32),
        ],
        scratch_shapes=[pltpu.VMEM((tm, D_MODEL), BF16)],
        compiler_params=_cparams(("parallel", "arbitrary")),
        name="inproj",
    )(x2d, gain, w_cat)


def _ssd_kernel(za_ref, xs_ref, bc_ref, sm_ref, cw_ref, cb_ref, dtb_ref, alog_ref,
                dsk_ref, ng_ref, y_ref, xpad, state, yscr):
    L = SSD_CHUNK
    c = pl.program_id(1)

    @pl.when(c == 0)
    def _():
        xpad[0:8, :] = jnp.zeros((8, 2 * 1024), F32)
        state[...] = jnp.zeros(state.shape, F32)

    xpad[8:8 + L, 0:1024] = xs_ref[...].astype(F32)
    xpad[8:8 + L, 1024:2048] = bc_ref[...].astype(F32)
    acc = cb_ref[...]
    for k in range(SSD_CONV):
        acc = acc + cw_ref[k:k + 1, :] * xpad[pl.ds(8 - (SSD_CONV - 1) + k, L), :]
    xbc = _silu(acc)
    xpad[0:8, :] = xpad[L:L + 8, :]

    lane = lax.broadcasted_iota(jnp.int32, (1, LANES), 1)
    head_lane = lane < SSD_HEADS
    dt = jax.nn.softplus(sm_ref[...] + dtb_ref[...])
    a = jnp.where(head_lane, -jnp.exp(alog_ref[...]), 0.0)
    adt = dt * a
    row_i = lax.broadcasted_iota(jnp.int32, (L, L), 0)
    col_i = lax.broadcasted_iota(jnp.int32, (L, L), 1)
    causal = row_i >= col_i
    tri = causal.astype(F32)
    cs = jnp.dot(tri, adt, preferred_element_type=F32, precision=lax.Precision.HIGHEST)
    cs_t = cs.T
    dt_t = dt.T

    xs_all = xbc[:, 0:SSD_D_INNER]
    for g in range(SSD_GROUPS):
        b_g = xbc[:, SSD_D_INNER + g * SSD_STATE:SSD_D_INNER + (g + 1) * SSD_STATE]
        c_g = xbc[:, SSD_D_INNER + (SSD_GROUPS + g) * SSD_STATE:
                  SSD_D_INNER + (SSD_GROUPS + g + 1) * SSD_STATE]
        c_g16 = c_g.astype(BF16)
        cb = lax.dot_general(c_g16, b_g.astype(BF16), (((1,), (1,)), ((), ())),
                             preferred_element_type=F32)
        for r in range(SSD_HEADS // SSD_GROUPS):
            h = g * (SSD_HEADS // SSD_GROUPS) + r
            col = cs[:, h:h + 1]
            row = cs_t[h:h + 1, :]
            last = cs_t[h:h + 1, L - 1:L]
            decay = jnp.exp(jnp.where(causal, col - row, MASK_VALUE))
            m = cb * decay * dt_t[h:h + 1, :]
            xs_h = xs_all[:, h * SSD_HEAD_DIM:(h + 1) * SSD_HEAD_DIM]
            xs_h16 = xs_h.astype(BF16)
            st = state[h]
            y_h = jnp.dot(m.astype(BF16), xs_h16, preferred_element_type=F32)
            y_off = lax.dot_general(c_g16, st.astype(BF16), (((1,), (1,)), ((), ())),
                                    preferred_element_type=F32)
            y_h = y_h + jnp.exp(col) * y_off
            w_in_state = jnp.exp(last - col) * dt[:, h:h + 1]
            bw = (b_g * w_in_state).astype(BF16)
            st_new = lax.dot_general(xs_h16, bw, (((0,), (0,)), ((), ())),
                                     preferred_element_type=F32)
            state[h] = st * jnp.exp(last) + st_new
            yscr[:, h * SSD_HEAD_DIM:(h + 1) * SSD_HEAD_DIM] = y_h

    y = yscr[...] + dsk_ref[...] * xs_all
    y = y * _silu(za_ref[...].astype(F32))
    y_ref[...] = _rms(y, ng_ref[...]).astype(BF16)


def _ssd(main, tail, conv_w, conv_b, dtb, alog, dsk, ng, bsz, seq):
    nc = seq // SSD_CHUNK
    L = SSD_CHUNK
    row = lambda b, c: b * nc + c
    vec = lambda n: pl.BlockSpec((1, n), lambda b, c: (0, 0))
    return pl.pallas_call(
        _ssd_kernel,
        grid=(bsz, nc),
        in_specs=[
            pl.BlockSpec((L, 1024), lambda b, c: (row(b, c), 0)),
            pl.BlockSpec((L, 1024), lambda b, c: (row(b, c), 1)),
            pl.BlockSpec((L, 1024), lambda b, c: (row(b, c), 2)),
            pl.BlockSpec((L, LANES), lambda b, c: (row(b, c), TAIL_SMALL // LANES)),
            pl.BlockSpec((SSD_CONV, 2048), lambda b, c: (0, 0)),
            vec(2048), vec(LANES), vec(LANES), vec(1024), vec(1024),
        ],
        out_specs=pl.BlockSpec((L, 1024), lambda b, c: (row(b, c), 0)),
        out_shape=jax.ShapeDtypeStruct((bsz * seq, SSD_D_INNER), BF16),
        scratch_shapes=[
            pltpu.VMEM((L + 8, 2048), F32),
            pltpu.VMEM((SSD_HEADS, SSD_HEAD_DIM, SSD_STATE), F32),
            pltpu.VMEM((L, SSD_D_INNER), F32),
        ],
        compiler_params=_cparams(("parallel", "arbitrary")),
        name="ssd",
    )(main, main, main, tail, conv_w, conv_b, dtb, alog, dsk, ng)


def _s5_disc(step, lr, li):
    mag = jnp.exp(lr * step)
    ang = li * step
    abar_re, abar_im = mag * jnp.cos(ang), mag * jnp.sin(ang)
    den = lr * lr + li * li
    nr, ni = abar_re - 1.0, abar_im
    f_re = (nr * lr + ni * li) / den
    f_im = (ni * lr - nr * li) / den
    return f_re, f_im


def _s5_pow(step, lr, li, k):
    mag = jnp.exp(lr * step * k)
    ang = li * step * k
    return mag * jnp.cos(ang), mag * jnp.sin(ang)


def _s5_prep_kernel(ls_ref, lr_ref, li_ref, br_ref, bi_ref, cr_ref, ci_ref,
                    toep_ref, bin_ref, cout_ref):
    step = jnp.exp(ls_ref[0])
    lr, li = lr_ref[0], li_ref[0]
    f_re, f_im = _s5_disc(step, lr, li)
    br, bi = br_ref[0], bi_ref[0]
    bb_re = f_re * br - f_im * bi
    bb_im = f_re * bi + f_im * br
    cr, ci = cr_ref[0], ci_ref[0]
    lag = (lax.broadcasted_iota(jnp.int32, (1, S5_K), 1) // S5_GROUP).astype(F32)

    p_re, p_im = _s5_pow(step, lr, li, lag)
    w_re = p_re * cr - p_im * ci
    w_im = p_re * ci + p_im * cr
    tn = (((0,), (0,)), ((), ()))
    hi = lax.Precision.HIGHEST
    kflat = (lax.dot_general(bb_re[:, 0:S5_GROUP], w_re, tn, precision=hi,
                             preferred_element_type=F32)
             - lax.dot_general(bb_im[:, 0:S5_GROUP], w_im, tn, precision=hi,
                               preferred_element_type=F32))
    lane = lax.broadcasted_iota(jnp.int32, (S5_GROUP, S5_K), 1)
    for s in range(S5_L):
        if s == 0:
            blk = kflat
        else:
            blk = jnp.where(lane >= s * S5_GROUP, pltpu.roll(kflat, s * S5_GROUP, 1), 0.0)
        toep_ref[0, s * S5_GROUP:(s + 1) * S5_GROUP, :] = blk.astype(BF16)

    q_re, q_im = _s5_pow(step, lr, li, (S5_L - 1.0) - lag)
    bin_ref[0, 0:S5_STATE, :] = (q_re * bb_re - q_im * bb_im).astype(BF16)
    bin_ref[0, S5_STATE:2 * S5_STATE, :] = (q_re * bb_im + q_im * bb_re).astype(BF16)

    r_re, r_im = _s5_pow(step, lr, li, lag + 1.0)
    cout_ref[0, 0:S5_STATE, :] = (r_re * cr - r_im * ci).astype(BF16)
    cout_ref[0, S5_STATE:2 * S5_STATE, :] = (-(r_re * ci + r_im * cr)).astype(BF16)


def _s5_prep(ls, lr, li, br_t, bi_t, cr_t, ci_t):
    g3 = lambda a, b: pl.BlockSpec((1, a, b), lambda g: (g, 0, 0))
    return pl.pallas_call(
        _s5_prep_kernel,
        grid=(S5_GROUPS,),
        in_specs=[g3(1, 1), g3(S5_STATE, 1), g3(S5_STATE, 1),
                  g3(S5_STATE, S5_K), g3(S5_STATE, S5_K), g3(S5_STATE, S5_K), g3(S5_STATE, S5_K)],
        out_specs=[g3(S5_K, S5_K), g3(2 * S5_STATE, S5_K), g3(2 * S5_STATE, S5_K)],
        out_shape=[jax.ShapeDtypeStruct((S5_GROUPS, S5_K, S5_K), BF16),
                   jax.ShapeDtypeStruct((S5_GROUPS, 2 * S5_STATE, S5_K), BF16),
                   jax.ShapeDtypeStruct((S5_GROUPS, 2 * S5_STATE, S5_K), BF16)],
        compiler_params=_cparams(("parallel",)),
        name="s5_prep",
    )(ls, lr, li, br_t, bi_t, cr_t, ci_t)


def _s5_state_kernel(u_ref, bin_ref, s_ref):
    s_ref[0] = lax.dot_general(u_ref[0], bin_ref[0], (((1,), (1,)), ((), ())),
                               preferred_element_type=F32)


def _s5_state(u_g, bin_t):
    g, r, _ = u_g.shape
    return pl.pallas_call(
        _s5_state_kernel,
        grid=(g,),
        in_specs=[pl.BlockSpec((1, r, S5_K), lambda i: (i, 0, 0)),
                  pl.BlockSpec((1, 2 * S5_STATE, S5_K), lambda i: (i, 0, 0))],
        out_specs=pl.BlockSpec((1, r, 2 * S5_STATE), lambda i: (i, 0, 0)),
        out_shape=jax.ShapeDtypeStruct((g, r, 2 * S5_STATE), F32),
        compiler_params=_cparams(("parallel",)),
        name="s5_state",
    )(u_g, bin_t)


S5_SCAN_GROUPS = 8


def _s5_scan_kernel(ls_ref, lr_ref, li_ref, s_ref, x_ref):
    gb, nchunk, bsz, _ = s_ref.shape
    step = jnp.exp(ls_ref[...])
    al_re, al_im = _s5_pow(step, lr_ref[...], li_ref[...], float(S5_L))
    a1 = jnp.concatenate([al_re, al_re], axis=-1)
    a2 = jnp.concatenate([-al_im, al_im], axis=-1)
    a1 = jnp.broadcast_to(a1, (gb, bsz, 2 * S5_STATE))
    a2 = jnp.broadcast_to(a2, (gb, bsz, 2 * S5_STATE))

    def body(c, carry):
        x, xs = carry
        x_ref[:, c] = x
        s = s_ref[:, c]
        ss = pltpu.roll(s, S5_STATE, 2)
        return a1 * x + a2 * xs + s, a1 * xs - a2 * x + ss

    zero = jnp.zeros((gb, bsz, 2 * S5_STATE), F32)
    lax.fori_loop(0, nchunk, body, (zero, zero))


def _s5_scan(ls, lr, li, s4):
    g, nchunk, bsz, n = s4.shape
    gb = S5_SCAN_GROUPS
    return pl.pallas_call(
        _s5_scan_kernel,
        grid=(g // gb,),
        in_specs=[pl.BlockSpec((gb, 1, 1), lambda i: (i, 0, 0)),
                  pl.BlockSpec((gb, 1, S5_STATE), lambda i: (i, 0, 0)),
                  pl.BlockSpec((gb, 1, S5_STATE), lambda i: (i, 0, 0)),
                  pl.BlockSpec((gb, nchunk, bsz, n), lambda i: (i, 0, 0, 0))],
        out_specs=pl.BlockSpec((gb, nchunk, bsz, n), lambda i: (i, 0, 0, 0)),
        out_shape=jax.ShapeDtypeStruct((g, nchunk, bsz, n), F32),
        compiler_params=_cparams(("parallel",)),
        name="s5_scan",
    )(ls, lr, li, s4)


def _s5_out_kernel(u_ref, toep_ref, xin_ref, cout_ref, y_ref):
    y = jnp.dot(u_ref[0], toep_ref[0], preferred_element_type=F32)
    y = y + jnp.dot(xin_ref[0].astype(BF16), cout_ref[0], preferred_element_type=F32)
    y_ref[0] = y.astype(BF16)


def _s5_out(u_g, toep, xin, cout):
    g, r, _ = u_g.shape
    return pl.pallas_call(
        _s5_out_kernel,
        grid=(g,),
        in_specs=[pl.BlockSpec((1, r, S5_K), lambda i: (i, 0, 0)),
                  pl.BlockSpec((1, S5_K, S5_K), lambda i: (i, 0, 0)),
                  pl.BlockSpec((1, r, 2 * S5_STATE), lambda i: (i, 0, 0)),
                  pl.BlockSpec((1, 2 * S5_STATE, S5_K), lambda i: (i, 0, 0))],
        out_specs=pl.BlockSpec((1, r, S5_K), lambda i: (i, 0, 0)),
        out_shape=jax.ShapeDtypeStruct((g, r, S5_K), BF16),
        compiler_params=_cparams(("parallel",)),
        name="s5_out",
    )(u_g, toep, xin, cout)


def _gelu_tanh(x):
    return 0.5 * x * (1.0 + jnp.tanh(0.7978845608028654 * (x + 0.044715 * (x * x * x))))


def _s5_post_kernel(y_ref, u_ref, z_ref, d_ref, w_ref, b_ref, o_ref):
    y = y_ref[...].astype(F32) + d_ref[...] * u_ref[...].astype(F32)
    gl = _gelu_tanh(y)
    gate = jnp.dot(gl.astype(BF16), w_ref[...], preferred_element_type=F32) + b_ref[...]
    out = gl * jax.nn.sigmoid(gate) * _silu(z_ref[...].astype(F32))
    o_ref[...] = out.astype(BF16)


def _s5_post(y_tok, main, d, w_glu, b_glu, tm):
    t = y_tok.shape[0]
    vec = pl.BlockSpec((1, 1024), lambda i: (0, 0))
    return pl.pallas_call(
        _s5_post_kernel,
        grid=(t // tm,),
        in_specs=[pl.BlockSpec((tm, 1024), lambda i: (i, 0)),
                  pl.BlockSpec((tm, 1024), lambda i: (i, 3)),
                  pl.BlockSpec((tm, 1024), lambda i: (i, 4)),
                  vec,
                  pl.BlockSpec((1024, 1024), lambda i: (0, 0)),
                  vec],
        out_specs=pl.BlockSpec((tm, 1024), lambda i: (i, 0)),
        out_shape=jax.ShapeDtypeStruct((t, S5_WIDTH), BF16),
        compiler_params=_cparams(("parallel",)),
        name="s5_post",
    )(y_tok, main, main, d, w_glu, b_glu)


def _rope_tables_kernel(f_ref, cos_ref, sa_ref, sb_ref):
    ts = cos_ref.shape[0]
    pos = (pl.program_id(0) * ts + lax.broadcasted_iota(jnp.int32, (ts, 1), 0)).astype(F32)
    ang = pos * f_ref[...]
    cos, sin = jnp.cos(ang), jnp.sin(ang)
    lane = lax.broadcasted_iota(jnp.int32, (1, HEAD_PAD), 1)
    half = MLA_ROPE // 2
    lo = (lane >= ROPE_LO) & (lane < ROPE_LO + half)
    hi = (lane >= ROPE_LO + half) & (lane < ROPE_LO + MLA_ROPE)
    cos_ref[...] = jnp.where(lane < ROPE_LO, 1.0, jnp.where(lo | hi, cos, 0.0))
    sa_ref[...] = jnp.where(lo, -sin, 0.0)
    sb_ref[...] = jnp.where(hi, sin, 0.0)


def _rope_tables(freq_lane, seq, ts):
    out = jax.ShapeDtypeStruct((seq, HEAD_PAD), F32)
    blk = pl.BlockSpec((ts, HEAD_PAD), lambda i: (i, 0))
    return pl.pallas_call(
        _rope_tables_kernel,
        grid=(seq // ts,),
        in_specs=[pl.BlockSpec((1, HEAD_PAD), lambda i: (0, 0))],
        out_specs=[blk, blk, blk],
        out_shape=[out, out, out],
        compiler_params=_cparams(("parallel",)),
        name="rope_tables",
    )(freq_lane)


def _rope(x, cos, sa, sb):
    half = MLA_ROPE // 2
    return (x * cos + pltpu.roll(x, HEAD_PAD - half, 1) * sa + pltpu.roll(x, half, 1) * sb)


def _mla_prep_kernel(cq_ref, ckv_ref, sm_ref, qn_ref, kvn_ref, wq_ref, wk_ref, wv_ref,
                     cos_ref, sa_ref, sb_ref, q_ref, k_ref, v_ref):
    cos, sa, sb = cos_ref[...], sa_ref[...], sb_ref[...]
    lane = lax.broadcasted_iota(jnp.int32, (1, HEAD_PAD), 1)
    scale = (MLA_NOPE + MLA_ROPE) ** -0.5
    qn = _rms(cq_ref[...], qn_ref[...]).astype(BF16)
    q = jnp.dot(qn, wq_ref[...], preferred_element_type=F32)
    kvn = _rms(ckv_ref[...], kvn_ref[...]).astype(BF16)
    k = jnp.dot(kvn, wk_ref[...], preferred_element_type=F32)
    v_ref[...] = jnp.dot(kvn, wv_ref[...], preferred_element_type=F32).astype(BF16)
    kr = _rope(jnp.where(lane >= ROPE_LO, sm_ref[...], 0.0),
               jnp.where(lane >= ROPE_LO, cos, 0.0), sa, sb)
    for h in range(MLA_HEADS):
        sl = slice(h * HEAD_PAD, (h + 1) * HEAD_PAD)
        q_ref[:, sl] = (_rope(q[:, sl], cos, sa, sb) * scale).astype(BF16)
        k_ref[:, sl] = (k[:, sl] + kr).astype(BF16)


def _mla_prep(tail, qn, kvn, wq, wk, wv, cos, sa, sb, seq, tm):
    t = tail.shape[0]
    npos = seq // tm
    tab = pl.BlockSpec((tm, HEAD_PAD), lambda i: (i % npos, 0))
    full = lambda a, b: pl.BlockSpec((a, b), lambda i: (0, 0))
    hw = MLA_HEADS * HEAD_PAD
    return pl.pallas_call(
        _mla_prep_kernel,
        grid=(t // tm,),
        in_specs=[pl.BlockSpec((tm, MLA_Q_RANK), lambda i: (i, TAIL_CQ // MLA_Q_RANK)),
                  pl.BlockSpec((tm, MLA_KV_RANK), lambda i: (i, TAIL_CKV // MLA_KV_RANK)),
                  pl.BlockSpec((tm, LANES), lambda i: (i, TAIL_SMALL // LANES)),
                  full(1, MLA_Q_RANK), full(1, MLA_KV_RANK),
                  full(MLA_Q_RANK, hw), full(MLA_KV_RANK, hw), full(MLA_KV_RANK, MLA_WIDTH),
                  tab, tab, tab],
        out_specs=[pl.BlockSpec((tm, hw), lambda i: (i, 0)),
                   pl.BlockSpec((tm, hw), lambda i: (i, 0)),
                   pl.BlockSpec((tm, MLA_WIDTH), lambda i: (i, 0))],
        out_shape=[jax.ShapeDtypeStruct((t, hw), BF16),
                   jax.ShapeDtypeStruct((t, hw), BF16),
                   jax.ShapeDtypeStruct((t, MLA_WIDTH), BF16)],
        compiler_params=_cparams(("parallel",)),
        name="mla_prep",
    )(tail, tail, tail, qn, kvn, wq, wk, wv, cos, sa, sb)


ATT_T = 512
ATT_HP = 2


def _attn_kernel(q_ref, k_ref, v_ref, o_ref):
    tq = ATT_T
    qi = pl.program_id(2)
    row_i = lax.broadcasted_iota(jnp.int32, (tq, tq), 0)
    col_i = lax.broadcasted_iota(jnp.int32, (tq, tq), 1)
    outs = []
    for hh in range(ATT_HP):
        q = q_ref[:, hh * HEAD_PAD:(hh + 1) * HEAD_PAD]

        def block(ki, carry, masked):
            m, l, acc = carry
            off = pl.multiple_of(ki * tq, tq)
            k = k_ref[pl.ds(off, tq), hh * HEAD_PAD:(hh + 1) * HEAD_PAD]
            v = v_ref[pl.ds(off, tq), hh * MLA_V:(hh + 1) * MLA_V]
            s = lax.dot_general(q, k, (((1,), (1,)), ((), ())), preferred_element_type=F32)
            if masked:
                s = jnp.where(col_i <= row_i, s, MASK_VALUE)
            m_new = jnp.maximum(m, jnp.max(s, axis=-1, keepdims=True))
            alpha = jnp.exp(m - m_new)
            p = jnp.exp(s - m_new)
            l_new = alpha * l + jnp.sum(p, axis=-1, keepdims=True)
            acc_new = alpha * acc + jnp.dot(p.astype(BF16), v, preferred_element_type=F32)
            return m_new, l_new, acc_new

        init = (jnp.full((tq, 1), MASK_VALUE, F32), jnp.zeros((tq, 1), F32),
                jnp.zeros((tq, MLA_V), F32))
        carry = lax.fori_loop(0, qi, functools.partial(block, masked=False), init)
        m, l, acc = block(qi, carry, True)
        outs.append(acc / l)
    o_ref[...] = jnp.concatenate(outs, axis=-1).astype(BF16)


def _mla_attn(q, k, v, bsz, seq):
    nq = seq // ATT_T
    nhp = MLA_HEADS // ATT_HP
    return pl.pallas_call(
        _attn_kernel,
        grid=(bsz, nhp, nq),
        in_specs=[pl.BlockSpec((ATT_T, ATT_HP * HEAD_PAD), lambda b, h, i: (b * nq + i, h)),
                  pl.BlockSpec((seq, ATT_HP * HEAD_PAD), lambda b, h, i: (b, h)),
                  pl.BlockSpec((seq, ATT_HP * MLA_V), lambda b, h, i: (b, h))],
        out_specs=pl.BlockSpec((ATT_T, ATT_HP * MLA_V), lambda b, h, i: (b * nq + i, h)),
        out_shape=jax.ShapeDtypeStruct((bsz * seq, MLA_WIDTH), BF16),
        compiler_params=_cparams(("parallel", "parallel", "arbitrary")),
        name="mla_attn",
    )(q, k, v)


def _merge_kernel(x_ref, ya_ref, yb_ref, o_ref, zc_ref, g0_ref, g1_ref, g2_ref,
                  wa_ref, wb_ref, wc_ref, wo_ref, pn_ref, out_ref):
    dot = lambda a, w: jnp.dot(a, w[...], preferred_element_type=F32)
    sig = lambda r: jax.nn.sigmoid(r[...].astype(F32))
    yc = (o_ref[...].astype(F32) * _silu(zc_ref[...].astype(F32))).astype(BF16)
    merged = (sig(g0_ref) * dot(ya_ref[...], wa_ref)
              + sig(g1_ref) * dot(yb_ref[...], wb_ref)
              + sig(g2_ref) * dot(yc, wc_ref))
    out = dot(merged.astype(BF16), wo_ref)
    out_ref[...] = x_ref[...] + _rms(out, pn_ref[...])


def _merge(x2d, ya, yb, o, main, wa, wb, wc, wo, pn, tm):
    t = x2d.shape[0]
    col = lambda j: pl.BlockSpec((tm, 1024), lambda i: (i, j))
    wfull = pl.BlockSpec((1024, 1024), lambda i: (0, 0))
    return pl.pallas_call(
        _merge_kernel,
        grid=(t // tm,),
        in_specs=[col(0), col(0), col(0), col(0), col(5), col(6), col(7), col(8),
                  wfull, wfull, wfull, wfull, pl.BlockSpec((1, 1024), lambda i: (0, 0))],
        out_specs=col(0),
        out_shape=jax.ShapeDtypeStruct((t, D_MODEL), F32),
        compiler_params=_cparams(("parallel",)),
        name="merge",
    )(x2d, ya, yb, o, main, main, main, main, wa, wb, wc, wo, pn)


def _pack_w_in(w):
    sizes = (SSD_D_INNER, SSD_D_INNER + 2 * SSD_GROUPS * SSD_STATE, SSD_HEADS, S5_WIDTH, S5_WIDTH,
             MLA_Q_RANK, MLA_KV_RANK, MLA_ROPE, MLA_WIDTH, 3 * D_MODEL)
    parts, start = [], 0
    for n in sizes:
        parts.append(w[:, start:start + n])
        start += n
    z_a, xbc, dt, u_b, z_b, c_q, c_kv, k_rope, z_c, gates = parts
    zeros = lambda n: jnp.zeros((w.shape[0], n), w.dtype)
    small = jnp.concatenate([dt, zeros(ROPE_LO - SSD_HEADS), k_rope,
                             zeros(LANES - ROPE_LO - MLA_ROPE)], axis=1)
    tail = jnp.concatenate([c_q, c_kv, small, zeros(1024 - TAIL_SMALL - LANES)], axis=1)
    return jnp.concatenate([z_a, xbc, u_b, z_b, z_c, gates, tail], axis=1).astype(BF16)


def _pad_heads(w, lo, width):
    r = w.shape[0]
    w3 = w.reshape(r, MLA_HEADS, -1)[:, :, lo:lo + width]
    w3 = jnp.pad(w3, ((0, 0), (0, 0), (0, HEAD_PAD - width)))
    return w3.reshape(r, MLA_HEADS * HEAD_PAD)


def _row(v, n=None):
    v = v.reshape(1, -1).astype(F32)
    if n is not None and v.shape[1] < n:
        v = jnp.pad(v, ((0, 0), (0, n - v.shape[1])))
    return v


def kernel(x, pre_norm, w_in, conv_w, conv_b, dt_bias, a_log, d_ssd, ssd_norm, w_a, s5_log_step, s5_lambda_re, s5_lambda_im, s5_b_re, s5_b_im, s5_c_re, s5_c_im, s5_d, w_glu, b_glu, w_b, q_norm, w_uq, kv_norm, w_ukv, w_c, w_o, post_norm):
    bsz, seq, _ = x.shape
    depth = w_in.shape[0]
    t = bsz * seq
    assert seq % ATT_T == 0 and seq % SSD_CHUNK == 0 and seq % S5_L == 0
    tm = min(1024, seq)
    nchunk = seq // S5_L

    half = MLA_ROPE // 2
    inv_freq = ROPE_THETA ** (-jnp.arange(0, MLA_ROPE, 2, dtype=F32) / MLA_ROPE)
    freq_lane = jnp.concatenate([jnp.zeros((ROPE_LO,), F32), inv_freq, inv_freq,
                                 jnp.zeros((HEAD_PAD - ROPE_LO - 2 * half,), F32)]).reshape(1, HEAD_PAD)
    cos, sa, sb = _rope_tables(freq_lane, seq, min(512, seq))

    x2d = x.reshape(t, D_MODEL)
    for l in range(depth):
        main, tail = _inproj(x2d, _row(pre_norm[l]), _pack_w_in(w_in[l]), tm)

        dsk = jnp.repeat(d_ssd[l].astype(F32), SSD_HEAD_DIM).reshape(1, SSD_D_INNER)
        y_a = _ssd(main, tail, conv_w[l].astype(F32), _row(conv_b[l]), _row(dt_bias[l], LANES),
                   _row(a_log[l], LANES), dsk, _row(ssd_norm[l]), bsz, seq)

        col3 = lambda a: a.astype(F32).reshape(S5_GROUPS, -1, 1)
        tile_l = lambda a: jnp.tile(a.astype(F32), (1, 1, S5_L))
        ls3 = s5_log_step[l].astype(F32).reshape(S5_GROUPS, 1, 1)
        toep, bin_t, cout = _s5_prep(
            ls3, col3(s5_lambda_re[l]), col3(s5_lambda_im[l]),
            tile_l(s5_b_re[l]), tile_l(s5_b_im[l]),
            tile_l(jnp.swapaxes(s5_c_re[l], 1, 2)), tile_l(jnp.swapaxes(s5_c_im[l], 1, 2)))
        u_tok = main[:, 3 * 1024:4 * 1024]
        u_g = (u_tok.reshape(bsz, nchunk, S5_L, S5_GROUPS, S5_GROUP)
               .transpose(3, 1, 0, 2, 4).reshape(S5_GROUPS, nchunk * bsz, S5_K))
        s_in = _s5_state(u_g, bin_t)
        row3 = lambda a: a.astype(F32).reshape(S5_GROUPS, 1, -1)
        xin = _s5_scan(ls3, row3(s5_lambda_re[l]), row3(s5_lambda_im[l]),
                       s_in.reshape(S5_GROUPS, nchunk, bsz, 2 * S5_STATE))
        y_g = _s5_out(u_g, toep, xin.reshape(S5_GROUPS, nchunk * bsz, 2 * S5_STATE), cout)
        y_tok = (y_g.reshape(S5_GROUPS, nchunk, bsz, S5_L, S5_GROUP)
                 .transpose(2, 1, 3, 0, 4).reshape(t, S5_WIDTH))
        y_b = _s5_post(y_tok, main, _row(s5_d[l]), w_glu[l].astype(BF16), _row(b_glu[l]), tm)

        stride_q = MLA_NOPE + MLA_ROPE
        wq = _pad_heads(w_uq[l], 0, stride_q).astype(BF16)
        wk = _pad_heads(w_ukv[l], 0, MLA_NOPE).astype(BF16)
        wv = (w_ukv[l].reshape(MLA_KV_RANK, MLA_HEADS, MLA_NOPE + MLA_V)[:, :, MLA_NOPE:]
              .reshape(MLA_KV_RANK, MLA_WIDTH).astype(BF16))
        q, k, v = _mla_prep(tail, _row(q_norm[l]), _row(kv_norm[l]), wq, wk, wv,
                            cos, sa, sb, seq, tm)
        o = _mla_attn(q, k, v, bsz, seq)

        x2d = _merge(x2d, y_a, y_b, o, main, w_a[l].astype(BF16), w_b[l].astype(BF16),
                     w_c[l].astype(BF16), w_o[l].astype(BF16), _row(post_norm[l]), min(512, seq))
    return x2d.reshape(bsz, seq, D_MODEL)
```

```python
import functools

import jax
import jax.numpy as jnp
from jax import lax
from jax.experimental import pallas as pl
from jax.experimental.pallas import tpu as pltpu

F32 = jnp.float32
BF16 = jnp.bfloat16

D_MODEL = 1024
SSD_D_INNER = 1024
SSD_HEAD_DIM = 64
SSD_HEADS = 16
SSD_GROUPS = 4
SSD_STATE = 128
SSD_CONV = 4
SSD_CHUNK = 128
S5_WIDTH = 1024
S5_GROUP = 16
S5_GROUPS = 64
S5_STATE = 64
MLA_HEADS = 16
MLA_NOPE = 64
MLA_ROPE = 32
MLA_V = 64
MLA_Q_RANK = 512
MLA_KV_RANK = 256
MLA_WIDTH = 1024
ROPE_THETA = 10000.0
EPS = 1e-6
MASK_VALUE = -1e30

LANES = 128
S5_L = 32
S5_K = S5_L * S5_GROUP
HEAD_PAD = 128
ROPE_LO = MLA_NOPE
VMEM_LIMIT = 56 * 1024 * 1024


def _cparams(sem):
    return pltpu.CompilerParams(dimension_semantics=sem, vmem_limit_bytes=VMEM_LIMIT)


def _silu(x):
    return x * jax.nn.sigmoid(x)


def _rms(x, g):
    return x * lax.rsqrt(jnp.mean(x * x, axis=-1, keepdims=True) + EPS) * g


N_MAIN = 9
TAIL_CQ, TAIL_CKV, TAIL_SMALL = 0, 512, 768


def _inproj_kernel(x_ref, g_ref, w_ref, main_ref, tail_ref, h_scr):
    j = pl.program_id(1)

    @pl.when(j == 0)
    def _():
        h_scr[...] = _rms(x_ref[...], g_ref[...]).astype(BF16)

    acc = jnp.dot(h_scr[...], w_ref[...], preferred_element_type=F32)

    @pl.when(j < N_MAIN)
    def _():
        main_ref[...] = acc.astype(BF16)

    @pl.when(j == N_MAIN)
    def _():
        tail_ref[...] = acc


def _inproj(x2d, gain, w_cat, tm):
    t = x2d.shape[0]
    return pl.pallas_call(
        _inproj_kernel,
        grid=(t // tm, N_MAIN + 1),
        in_specs=[
            pl.BlockSpec((tm, D_MODEL), lambda i, j: (i, 0)),
            pl.BlockSpec((1, D_MODEL), lambda i, j: (0, 0)),
            pl.BlockSpec((D_MODEL, 1024), lambda i, j: (0, j)),
        ],
        out_specs=[
            pl.BlockSpec((tm, 1024), lambda i, j: (i, jnp.minimum(j, N_MAIN - 1))),
            pl.BlockSpec((tm, 1024), lambda i, j: (i, 0)),
        ],
        out_shape=[
            jax.ShapeDtypeStruct((t, N_MAIN * 1024), BF16),
            jax.ShapeDtypeStruct((t, 1024), F32),
        ],
        scratch_shapes=[pltpu.VMEM((tm, D_MODEL), BF16)],
        compiler_params=_cparams(("parallel", "arbitrary")),
        name="inproj",
    )(x2d, gain, w_cat)


def _ssd_kernel(za_ref, xs_ref, bc_ref, sm_ref, cw_ref, cb_ref, dtb_ref, alog_ref,
                dsk_ref, ng_ref, y_ref, xpad, state, yscr):
    L = SSD_CHUNK
    c = pl.program_id(1)

    @pl.when(c == 0)
    def _():
        xpad[0:8, :] = jnp.zeros((8, 2 * 1024), F32)
        state[...] = jnp.zeros(state.shape, F32)

    xpad[8:8 + L, 0:1024] = xs_ref[...].astype(F32)
    xpad[8:8 + L, 1024:2048] = bc_ref[...].astype(F32)
    acc = cb_ref[...]
    for k in range(SSD_CONV):
        acc = acc + cw_ref[k:k + 1, :] * xpad[pl.ds(8 - (SSD_CONV - 1) + k, L), :]
    xbc = _silu(acc)
    xpad[0:8, :] = xpad[L:L + 8, :]

    lane = lax.broadcasted_iota(jnp.int32, (1, LANES), 1)
    head_lane = lane < SSD_HEADS
    dt = jax.nn.softplus(sm_ref[...] + dtb_ref[...])
    a = jnp.where(head_lane, -jnp.exp(alog_ref[...]), 0.0)
    adt = dt * a
    row_i = lax.broadcasted_iota(jnp.int32, (L, L), 0)
    col_i = lax.broadcasted_iota(jnp.int32, (L, L), 1)
    causal = row_i >= col_i
    tri = causal.astype(F32)
    cs = jnp.dot(tri, adt, preferred_element_type=F32, precision=lax.Precision.HIGHEST)
    cs_t = cs.T
    dt_t = dt.T

    xs_all = xbc[:, 0:SSD_D_INNER]
    for g in range(SSD_GROUPS):
        b_g = xbc[:, SSD_D_INNER + g * SSD_STATE:SSD_D_INNER + (g + 1) * SSD_STATE]
        c_g = xbc[:, SSD_D_INNER + (SSD_GROUPS + g) * SSD_STATE:
                  SSD_D_INNER + (SSD_GROUPS + g + 1) * SSD_STATE]
        c_g16 = c_g.astype(BF16)
        cb = lax.dot_general(c_g16, b_g.astype(BF16), (((1,), (1,)), ((), ())),
                             preferred_element_type=F32)
        for r in range(SSD_HEADS // SSD_GROUPS):
            h = g * (SSD_HEADS // SSD_GROUPS) + r
            col = cs[:, h:h + 1]
            row = cs_t[h:h + 1, :]
            last = cs_t[h:h + 1, L - 1:L]
            decay = jnp.exp(jnp.where(causal, col - row, MASK_VALUE))
            m = cb * decay * dt_t[h:h + 1, :]
            xs_h = xs_all[:, h * SSD_HEAD_DIM:(h + 1) * SSD_HEAD_DIM]
            xs_h16 = xs_h.astype(BF16)
            st = state[h]
            y_h = jnp.dot(m.astype(BF16), xs_h16, preferred_element_type=F32)
            y_off = lax.dot_general(c_g16, st.astype(BF16), (((1,), (1,)), ((), ())),
                                    preferred_element_type=F32)
            y_h = y_h + jnp.exp(col) * y_off
            w_in_state = jnp.exp(last - col) * dt[:, h:h + 1]
            bw = (b_g * w_in_state).astype(BF16)
            st_new = lax.dot_general(xs_h16, bw, (((0,), (0,)), ((), ())),
                                     preferred_element_type=F32)
            state[h] = st * jnp.exp(last) + st_new
            yscr[:, h * SSD_HEAD_DIM:(h + 1) * SSD_HEAD_DIM] = y_h

    y = yscr[...] + dsk_ref[...] * xs_all
    y = y * _silu(za_ref[...].astype(F32))
    y_ref[...] = _rms(y, ng_ref[...]).astype(BF16)


def _ssd(main, tail, conv_w, conv_b, dtb, alog, dsk, ng, bsz, seq):
    nc = seq // SSD_CHUNK
    L = SSD_CHUNK
    row = lambda b, c: b * nc + c
    vec = lambda n: pl.BlockSpec((1, n), lambda b, c: (0, 0))
    return pl.pallas_call(
        _ssd_kernel,
        grid=(bsz, nc),
        in_specs=[
            pl.BlockSpec((L, 1024), lambda b, c: (row(b, c), 0)),
            pl.BlockSpec((L, 1024), lambda b, c: (row(b, c), 1)),
            pl.BlockSpec((L, 1024), lambda b, c: (row(b, c), 2)),
            pl.BlockSpec((L, LANES), lambda b, c: (row(b, c), TAIL_SMALL // LANES)),
            pl.BlockSpec((SSD_CONV, 2048), lambda b, c: (0, 0)),
            vec(2048), vec(LANES), vec(LANES), vec(1024), vec(1024),
        ],
        out_specs=pl.BlockSpec((L, 1024), lambda b, c: (row(b, c), 0)),
        out_shape=jax.ShapeDtypeStruct((bsz * seq, SSD_D_INNER), BF16),
        scratch_shapes=[
            pltpu.VMEM((L + 8, 2048), F32),
            pltpu.VMEM((SSD_HEADS, SSD_HEAD_DIM, SSD_STATE), F32),
            pltpu.VMEM((L, SSD_D_INNER), F32),
        ],
        compiler_params=_cparams(("parallel", "arbitrary")),
        name="ssd",
    )(main, main, main, tail, conv_w, conv_b, dtb, alog, dsk, ng)


def _s5_disc(step, lr, li):
    mag = jnp.exp(lr * step)
    ang = li * step
    abar_re, abar_im = mag * jnp.cos(ang), mag * jnp.sin(ang)
    den = lr * lr + li * li
    nr, ni = abar_re - 1.0, abar_im
    f_re = (nr * lr + ni * li) / den
    f_im = (ni * lr - nr * li) / den
    return f_re, f_im


def _s5_pow(step, lr, li, k):
    mag = jnp.exp(lr * step * k)
    ang = li * step * k
    return mag * jnp.cos(ang), mag * jnp.sin(ang)


def _s5_prep_kernel(ls_ref, lr_ref, li_ref, br_ref, bi_ref, cr_ref, ci_ref,
                    toep_ref, bin_ref, cout_ref):
    step = jnp.exp(ls_ref[0])
    lr, li = lr_ref[0], li_ref[0]
    f_re, f_im = _s5_disc(step, lr, li)
    br, bi = br_ref[0], bi_ref[0]
    bb_re = f_re * br - f_im * bi
    bb_im = f_re * bi + f_im * br
    cr, ci = cr_ref[0], ci_ref[0]
    lag = (lax.broadcasted_iota(jnp.int32, (1, S5_K), 1) // S5_GROUP).astype(F32)

    p_re, p_im = _s5_pow(step, lr, li, lag)
    w_re = p_re * cr - p_im * ci
    w_im = p_re * ci + p_im * cr
    tn = (((0,), (0,)), ((), ()))
    hi = lax.Precision.HIGHEST
    kflat = (lax.dot_general(bb_re[:, 0:S5_GROUP], w_re, tn, precision=hi,
                             preferred_element_type=F32)
             - lax.dot_general(bb_im[:, 0:S5_GROUP], w_im, tn, precision=hi,
                               preferred_element_type=F32))
    lane = lax.broadcasted_iota(jnp.int32, (S5_GROUP, S5_K), 1)
    for s in range(S5_L):
        if s == 0:
            blk = kflat
        else:
            blk = jnp.where(lane >= s * S5_GROUP, pltpu.roll(kflat, s * S5_GROUP, 1), 0.0)
        toep_ref[0, s * S5_GROUP:(s + 1) * S5_GROUP, :] = blk.astype(BF16)

    q_re, q_im = _s5_pow(step, lr, li, (S5_L - 1.0) - lag)
    bin_ref[0, 0:S5_STATE, :] = (q_re * bb_re - q_im * bb_im).astype(BF16)
    bin_ref[0, S5_STATE:2 * S5_STATE, :] = (q_re * bb_im + q_im * bb_re).astype(BF16)

    r_re, r_im = _s5_pow(step, lr, li, lag + 1.0)
    cout_ref[0, 0:S5_STATE, :] = (r_re * cr - r_im * ci).astype(BF16)
    cout_ref[0, S5_STATE:2 * S5_STATE, :] = (-(r_re * ci + r_im * cr)).astype(BF16)


def _s5_prep(ls, lr, li, br_t, bi_t, cr_t, ci_t):
    g3 = lambda a, b: pl.BlockSpec((1, a, b), lambda g: (g, 0, 0))
    return pl.pallas_call(
        _s5_prep_kernel,
        grid=(S5_GROUPS,),
        in_specs=[g3(1, 1), g3(S5_STATE, 1), g3(S5_STATE, 1),
                  g3(S5_STATE, S5_K), g3(S5_STATE, S5_K), g3(S5_STATE, S5_K), g3(S5_STATE, S5_K)],
        out_specs=[g3(S5_K, S5_K), g3(2 * S5_STATE, S5_K), g3(2 * S5_STATE, S5_K)],
        out_shape=[jax.ShapeDtypeStruct((S5_GROUPS, S5_K, S5_K), BF16),
                   jax.ShapeDtypeStruct((S5_GROUPS, 2 * S5_STATE, S5_K), BF16),
                   jax.ShapeDtypeStruct((S5_GROUPS, 2 * S5_STATE, S5_K), BF16)],
        compiler_params=_cparams(("parallel",)),
        name="s5_prep",
    )(ls, lr, li, br_t, bi_t, cr_t, ci_t)


def _s5_state_kernel(u_ref, bin_ref, s_ref):
    s_ref[0] = lax.dot_general(u_ref[0], bin_ref[0], (((1,), (1,)), ((), ())),
                               preferred_element_type=F32)


def _s5_state(u_g, bin_t):
    g, r, _ = u_g.shape
    return pl.pallas_call(
        _s5_state_kernel,
        grid=(g,),
        in_specs=[pl.BlockSpec((1, r, S5_K), lambda i: (i, 0, 0)),
                  pl.BlockSpec((1, 2 * S5_STATE, S5_K), lambda i: (i, 0, 0))],
        out_specs=pl.BlockSpec((1, r, 2 * S5_STATE), lambda i: (i, 0, 0)),
        out_shape=jax.ShapeDtypeStruct((g, r, 2 * S5_STATE), F32),
        compiler_params=_cparams(("parallel",)),
        name="s5_state",
    )(u_g, bin_t)


S5_SCAN_GROUPS = 8


def _s5_scan_kernel(ls_ref, lr_ref, li_ref, s_ref, x_ref):
    gb, nchunk, bsz, _ = s_ref.shape
    step = jnp.exp(ls_ref[...])
    al_re, al_im = _s5_pow(step, lr_ref[...], li_ref[...], float(S5_L))
    a1 = jnp.concatenate([al_re, al_re], axis=-1)
    a2 = jnp.concatenate([-al_im, al_im], axis=-1)
    a1 = jnp.broadcast_to(a1, (gb, bsz, 2 * S5_STATE))
    a2 = jnp.broadcast_to(a2, (gb, bsz, 2 * S5_STATE))

    def body(c, carry):
        x, xs = carry
        x_ref[:, c] = x
        s = s_ref[:, c]
        ss = pltpu.roll(s, S5_STATE, 2)
        return a1 * x + a2 * xs + s, a1 * xs - a2 * x + ss

    zero = jnp.zeros((gb, bsz, 2 * S5_STATE), F32)
    lax.fori_loop(0, nchunk, body, (zero, zero))


def _s5_scan(ls, lr, li, s4):
    g, nchunk, bsz, n = s4.shape
    gb = S5_SCAN_GROUPS
    return pl.pallas_call(
        _s5_scan_kernel,
        grid=(g // gb,),
        in_specs=[pl.BlockSpec((gb, 1, 1), lambda i: (i, 0, 0)),
                  pl.BlockSpec((gb, 1, S5_STATE), lambda i: (i, 0, 0)),
                  pl.BlockSpec((gb, 1, S5_STATE), lambda i: (i, 0, 0)),
                  pl.BlockSpec((gb, nchunk, bsz, n), lambda i: (i, 0, 0, 0))],
        out_specs=pl.BlockSpec((gb, nchunk, bsz, n), lambda i: (i, 0, 0, 0)),
        out_shape=jax.ShapeDtypeStruct((g, nchunk, bsz, n), F32),
        compiler_params=_cparams(("parallel",)),
        name="s5_scan",
    )(ls, lr, li, s4)


def _s5_out_kernel(u_ref, toep_ref, xin_ref, cout_ref, y_ref):
    y = jnp.dot(u_ref[0], toep_ref[0], preferred_element_type=F32)
    y = y + jnp.dot(xin_ref[0].astype(BF16), cout_ref[0], preferred_element_type=F32)
    y_ref[0] = y.astype(BF16)


def _s5_out(u_g, toep, xin, cout):
    g, r, _ = u_g.shape
    return pl.pallas_call(
        _s5_out_kernel,
        grid=(g,),
        in_specs=[pl.BlockSpec((1, r, S5_K), lambda i: (i, 0, 0)),
                  pl.BlockSpec((1, S5_K, S5_K), lambda i: (i, 0, 0)),
                  pl.BlockSpec((1, r, 2 * S5_STATE), lambda i: (i, 0, 0)),
                  pl.BlockSpec((1, 2 * S5_STATE, S5_K), lambda i: (i, 0, 0))],
        out_specs=pl.BlockSpec((1, r, S5_K), lambda i: (i, 0, 0)),
        out_shape=jax.ShapeDtypeStruct((g, r, S5_K), BF16),
        compiler_params=_cparams(("parallel",)),
        name="s5_out",
    )(u_g, toep, xin, cout)


def _gelu_tanh(x):
    return 0.5 * x * (1.0 + jnp.tanh(0.7978845608028654 * (x + 0.044715 * (x * x * x))))


def _s5_post_kernel(y_ref, u_ref, z_ref, d_ref, w_ref, b_ref, o_ref):
    y = y_ref[...].astype(F32) + d_ref[...] * u_ref[...].astype(F32)
    gl = _gelu_tanh(y)
    gate = jnp.dot(gl.astype(BF16), w_ref[...], preferred_element_type=F32) + b_ref[...]
    out = gl * jax.nn.sigmoid(gate) * _silu(z_ref[...].astype(F32))
    o_ref[...] = out.astype(BF16)


def _s5_post(y_tok, main, d, w_glu, b_glu, tm):
    t = y_tok.shape[0]
    vec = pl.BlockSpec((1, 1024), lambda i: (0, 0))
    return pl.pallas_call(
        _s5_post_kernel,
        grid=(t // tm,),
        in_specs=[pl.BlockSpec((tm, 1024), lambda i: (i, 0)),
                  pl.BlockSpec((tm, 1024), lambda i: (i, 3)),
                  pl.BlockSpec((tm, 1024), lambda i: (i, 4)),
                  vec,
                  pl.BlockSpec((1024, 1024), lambda i: (0, 0)),
                  vec],
        out_specs=pl.BlockSpec((tm, 1024), lambda i: (i, 0)),
        out_shape=jax.ShapeDtypeStruct((t, S5_WIDTH), BF16),
        compiler_params=_cparams(("parallel",)),
        name="s5_post",
    )(y_tok, main, main, d, w_glu, b_glu)


def _rope_tables_kernel(f_ref, fcol_ref, cos_ref, sa_ref, sb_ref, cost_ref, sint_ref):
    ts = cos_ref.shape[0]
    base = pl.program_id(0) * ts
    pos = (base + lax.broadcasted_iota(jnp.int32, (ts, 1), 0)).astype(F32)
    ang = pos * f_ref[...]
    cos, sin = jnp.cos(ang), jnp.sin(ang)
    lane = lax.broadcasted_iota(jnp.int32, (1, HEAD_PAD), 1)
    half = MLA_ROPE // 2
    lo = (lane >= ROPE_LO) & (lane < ROPE_LO + half)
    hi = (lane >= ROPE_LO + half) & (lane < ROPE_LO + MLA_ROPE)
    cos_ref[...] = jnp.where(lo | hi, cos, 0.0)
    sa_ref[...] = jnp.where(lo, -sin, 0.0)
    sb_ref[...] = jnp.where(hi, sin, 0.0)
    pos_t = (base + lax.broadcasted_iota(jnp.int32, (1, ts), 1)).astype(F32)
    ang_t = fcol_ref[...] * pos_t
    cost_ref[...] = jnp.cos(ang_t)
    sint_ref[...] = jnp.sin(ang_t)


def _rope_tables(freq_lane, freq_col, seq, ts):
    half = MLA_ROPE // 2
    out = jax.ShapeDtypeStruct((seq, HEAD_PAD), F32)
    out_t = jax.ShapeDtypeStruct((half, seq), F32)
    blk = pl.BlockSpec((ts, HEAD_PAD), lambda i: (i, 0))
    blk_t = pl.BlockSpec((half, ts), lambda i: (0, i))
    return pl.pallas_call(
        _rope_tables_kernel,
        grid=(seq // ts,),
        in_specs=[pl.BlockSpec((1, HEAD_PAD), lambda i: (0, 0)),
                  pl.BlockSpec((half, 1), lambda i: (0, 0))],
        out_specs=[blk, blk, blk, blk_t, blk_t],
        out_shape=[out, out, out, out_t, out_t],
        compiler_params=_cparams(("parallel",)),
        name="rope_tables",
    )(freq_lane, freq_col)


def _rope(x, cos, sa, sb):
    half = MLA_ROPE // 2
    return (x * cos + pltpu.roll(x, HEAD_PAD - half, 1) * sa + pltpu.roll(x, half, 1) * sb)


ATT_T = 512
ATT_HP = 2
LOG2E = 1.4426950408889634


def _mla_prep_kernel(cq_ref, ckv_ref, sm_ref, qn_ref, kvn_ref, wqt_ref, wk_ref, wvt_ref,
                     cos_ref, sa_ref, sb_ref, cost_ref, sint_ref, qt_ref, k_ref, vt_ref):
    tm = cq_ref.shape[0]
    half = MLA_ROPE // 2
    nt = (((1,), (1,)), ((), ()))
    scale = (MLA_NOPE + MLA_ROPE) ** -0.5 * LOG2E
    qn = _rms(cq_ref[...], qn_ref[...]).astype(BF16)
    q_t = lax.dot_general(wqt_ref[...], qn, nt, preferred_element_type=F32)
    c_t, s_t = cost_ref[...], sint_ref[...]
    for h in range(MLA_HEADS):
        r0 = h * HEAD_PAD
        r1 = r0 + ROPE_LO
        qt_ref[r0:r1, :] = (q_t[r0:r1, :] * scale).astype(BF16)
        x1, x2 = q_t[r1:r1 + half, :], q_t[r1 + half:r1 + 2 * half, :]
        qt_ref[r1:r1 + half, :] = ((x1 * c_t - x2 * s_t) * scale).astype(BF16)
        qt_ref[r1 + half:r1 + 2 * half, :] = ((x1 * s_t + x2 * c_t) * scale).astype(BF16)
        qt_ref[r1 + 2 * half:r0 + HEAD_PAD, :] = jnp.zeros((HEAD_PAD - ROPE_LO - 2 * half, tm), BF16)

    kvn = _rms(ckv_ref[...], kvn_ref[...]).astype(BF16)
    k = jnp.dot(kvn, wk_ref[...], preferred_element_type=F32)
    kr = _rope(sm_ref[...], cos_ref[...], sa_ref[...], sb_ref[...])
    for h in range(MLA_HEADS):
        sl = slice(h * HEAD_PAD, (h + 1) * HEAD_PAD)
        k_ref[:, sl] = (k[:, sl] + kr).astype(BF16)
    v_t = lax.dot_general(wvt_ref[...], kvn, nt, preferred_element_type=F32)
    for j in range(tm // ATT_T):
        vt_ref[j] = v_t[:, j * ATT_T:(j + 1) * ATT_T].astype(BF16)


def _mla_prep(tail, qn, kvn, wqt, wk, wvt, tables, seq, tm):
    cos, sa, sb, cost, sint = tables
    t = tail.shape[0]
    npos = seq // tm
    half = MLA_ROPE // 2
    tab = pl.BlockSpec((tm, HEAD_PAD), lambda i: (i % npos, 0))
    tab_t = pl.BlockSpec((half, tm), lambda i: (0, i % npos))
    full = lambda a, b: pl.BlockSpec((a, b), lambda i: (0, 0))
    hw = MLA_HEADS * HEAD_PAD
    nkv = tm // ATT_T
    return pl.pallas_call(
        _mla_prep_kernel,
        grid=(t // tm,),
        in_specs=[pl.BlockSpec((tm, MLA_Q_RANK), lambda i: (i, TAIL_CQ // MLA_Q_RANK)),
                  pl.BlockSpec((tm, MLA_KV_RANK), lambda i: (i, TAIL_CKV // MLA_KV_RANK)),
                  pl.BlockSpec((tm, LANES), lambda i: (i, TAIL_SMALL // LANES)),
                  full(1, MLA_Q_RANK), full(1, MLA_KV_RANK),
                  full(hw, MLA_Q_RANK), full(MLA_KV_RANK, hw), full(MLA_WIDTH, MLA_KV_RANK),
                  tab, tab, tab, tab_t, tab_t],
        out_specs=[pl.BlockSpec((hw, tm), lambda i: (0, i)),
                   pl.BlockSpec((tm, hw), lambda i: (i, 0)),
                   pl.BlockSpec((nkv, MLA_WIDTH, ATT_T), lambda i: (i, 0, 0))],
        out_shape=[jax.ShapeDtypeStruct((hw, t), BF16),
                   jax.ShapeDtypeStruct((t, hw), BF16),
                   jax.ShapeDtypeStruct((t // ATT_T, MLA_WIDTH, ATT_T), BF16)],
        compiler_params=_cparams(("parallel",)),
        name="mla_prep",
    )(tail, tail, tail, qn, kvn, wqt, wk, wvt, cos, sa, sb, cost, sint)


def _attn_kernel(qt_ref, k_ref, vt_ref, o_ref, acc_scr, m_scr, l_scr, s0, s1, mb0, mb1):
    t = ATT_T
    qi = pl.program_id(2)
    m_scr[...] = jnp.full(m_scr.shape, MASK_VALUE, F32)
    l_scr[...] = jnp.zeros(l_scr.shape, F32)
    acc_scr[...] = jnp.zeros(acc_scr.shape, F32)

    def scores(kidx, s_buf, mb_buf, masked):
        off = pl.multiple_of(kidx * t, t)
        for hh in range(ATT_HP):
            q_t = qt_ref[hh * HEAD_PAD:(hh + 1) * HEAD_PAD, :]
            k = k_ref[pl.ds(off, t), hh * HEAD_PAD:(hh + 1) * HEAD_PAD]
            s = jnp.dot(k, q_t, preferred_element_type=F32)
            if masked:
                kv_i = lax.broadcasted_iota(jnp.int32, (t, t), 0)
                q_i = lax.broadcasted_iota(jnp.int32, (t, t), 1)
                s = jnp.where(kv_i <= q_i, s, MASK_VALUE)
            s_buf[hh] = s
            mb_buf[hh:hh + 1, :] = jnp.max(s, axis=0, keepdims=True)

    def update(vidx, s_buf, mb_buf):
        for hh in range(ATT_HP):
            rows = slice(hh * MLA_V, (hh + 1) * MLA_V)
            m_old = m_scr[hh:hh + 1, :]
            m_new = jnp.maximum(m_old, mb_buf[hh:hh + 1, :])
            alpha = jnp.exp2(m_old - m_new)
            p = jnp.exp2(s_buf[hh] - m_new)
            l_scr[hh:hh + 1, :] = alpha * l_scr[hh:hh + 1, :] + jnp.sum(p, axis=0, keepdims=True)
            m_scr[hh:hh + 1, :] = m_new
            pv = jnp.dot(vt_ref[vidx, rows, :], p.astype(BF16), preferred_element_type=F32)
            acc_scr[rows, :] = alpha * acc_scr[rows, :] + pv

    block_at = lambda pos: jnp.where(pos == 0, qi, pos - 1)
    scores(qi, s0, mb0, True)

    def pair(j, carry):
        scores(2 * j, s1, mb1, False)
        update(block_at(2 * j), s0, mb0)
        scores(2 * j + 1, s0, mb0, False)
        update(2 * j, s1, mb1)
        return carry

    lax.fori_loop(0, qi // 2, pair, 0)

    @pl.when(qi % 2 == 1)
    def _():
        scores(qi - 1, s1, mb1, False)
        update(block_at(qi - 1), s0, mb0)
        update(qi - 1, s1, mb1)

    @pl.when(qi % 2 == 0)
    def _():
        update(block_at(qi), s0, mb0)

    for hh in range(ATT_HP):
        rows = slice(hh * MLA_V, (hh + 1) * MLA_V)
        acc_scr[rows, :] = acc_scr[rows, :] / l_scr[hh:hh + 1, :]
    o_ref[...] = acc_scr[...].T.astype(BF16)


def _mla_attn(q_t, k, v_t, bsz, seq):
    nq = seq // ATT_T
    nhp = MLA_HEADS // ATT_HP
    return pl.pallas_call(
        _attn_kernel,
        grid=(bsz, nhp, nq),
        in_specs=[pl.BlockSpec((ATT_HP * HEAD_PAD, ATT_T), lambda b, h, i: (h, b * nq + i)),
                  pl.BlockSpec((seq, ATT_HP * HEAD_PAD), lambda b, h, i: (b, h)),
                  pl.BlockSpec((nq, ATT_HP * MLA_V, ATT_T), lambda b, h, i: (b, h, 0))],
        out_specs=pl.BlockSpec((ATT_T, ATT_HP * MLA_V), lambda b, h, i: (b * nq + i, h)),
        out_shape=jax.ShapeDtypeStruct((bsz * seq, MLA_WIDTH), BF16),
        scratch_shapes=[pltpu.VMEM((ATT_HP * MLA_V, ATT_T), F32),
                        pltpu.VMEM((8, ATT_T), F32),
                        pltpu.VMEM((8, ATT_T), F32),
                        pltpu.VMEM((ATT_HP, ATT_T, ATT_T), F32),
                        pltpu.VMEM((ATT_HP, ATT_T, ATT_T), F32),
                        pltpu.VMEM((8, ATT_T), F32),
                        pltpu.VMEM((8, ATT_T), F32)],
        compiler_params=_cparams(("parallel", "parallel", "arbitrary")),
        name="mla_attn",
    )(q_t, k, v_t)


def _merge_kernel(x_ref, ya_ref, yb_ref, o_ref, zc_ref, g0_ref, g1_ref, g2_ref,
                  wa_ref, wb_ref, wc_ref, wo_ref, pn_ref, out_ref):
    dot = lambda a, w: jnp.dot(a, w[...], preferred_element_type=F32)
    sig = lambda r: jax.nn.sigmoid(r[...].astype(F32))
    yc = (o_ref[...].astype(F32) * _silu(zc_ref[...].astype(F32))).astype(BF16)
    merged = (sig(g0_ref) * dot(ya_ref[...], wa_ref)
              + sig(g1_ref) * dot(yb_ref[...], wb_ref)
              + sig(g2_ref) * dot(yc, wc_ref))
    out = dot(merged.astype(BF16), wo_ref)
    out_ref[...] = x_ref[...] + _rms(out, pn_ref[...])


def _merge(x2d, ya, yb, o, main, wa, wb, wc, wo, pn, tm):
    t = x2d.shape[0]
    col = lambda j: pl.BlockSpec((tm, 1024), lambda i: (i, j))
    wfull = pl.BlockSpec((1024, 1024), lambda i: (0, 0))
    return pl.pallas_call(
        _merge_kernel,
        grid=(t // tm,),
        in_specs=[col(0), col(0), col(0), col(0), col(5), col(6), col(7), col(8),
                  wfull, wfull, wfull, wfull, pl.BlockSpec((1, 1024), lambda i: (0, 0))],
        out_specs=col(0),
        out_shape=jax.ShapeDtypeStruct((t, D_MODEL), F32),
        compiler_params=_cparams(("parallel",)),
        name="merge",
    )(x2d, ya, yb, o, main, main, main, main, wa, wb, wc, wo, pn)


def _pack_w_in(w):
    sizes = (SSD_D_INNER, SSD_D_INNER + 2 * SSD_GROUPS * SSD_STATE, SSD_HEADS, S5_WIDTH, S5_WIDTH,
             MLA_Q_RANK, MLA_KV_RANK, MLA_ROPE, MLA_WIDTH, 3 * D_MODEL)
    parts, start = [], 0
    for n in sizes:
        parts.append(w[:, start:start + n])
        start += n
    z_a, xbc, dt, u_b, z_b, c_q, c_kv, k_rope, z_c, gates = parts
    zeros = lambda n: jnp.zeros((w.shape[0], n), w.dtype)
    small = jnp.concatenate([dt, zeros(ROPE_LO - SSD_HEADS), k_rope,
                             zeros(LANES - ROPE_LO - MLA_ROPE)], axis=1)
    tail = jnp.concatenate([c_q, c_kv, small, zeros(1024 - TAIL_SMALL - LANES)], axis=1)
    return jnp.concatenate([z_a, xbc, u_b, z_b, z_c, gates, tail], axis=1).astype(BF16)


def _pad_heads(w, lo, width):
    r = w.shape[0]
    w3 = w.reshape(r, MLA_HEADS, -1)[:, :, lo:lo + width]
    w3 = jnp.pad(w3, ((0, 0), (0, 0), (0, HEAD_PAD - width)))
    return w3.reshape(r, MLA_HEADS * HEAD_PAD)


def _row(v, n=None):
    v = v.reshape(1, -1).astype(F32)
    if n is not None and v.shape[1] < n:
        v = jnp.pad(v, ((0, 0), (0, n - v.shape[1])))
    return v


def kernel(x, pre_norm, w_in, conv_w, conv_b, dt_bias, a_log, d_ssd, ssd_norm, w_a, s5_log_step, s5_lambda_re, s5_lambda_im, s5_b_re, s5_b_im, s5_c_re, s5_c_im, s5_d, w_glu, b_glu, w_b, q_norm, w_uq, kv_norm, w_ukv, w_c, w_o, post_norm):
    bsz, seq, _ = x.shape
    depth = w_in.shape[0]
    t = bsz * seq
    assert seq % ATT_T == 0 and seq % SSD_CHUNK == 0 and seq % S5_L == 0
    tm = min(1024, seq)
    nchunk = seq // S5_L

    half = MLA_ROPE // 2
    inv_freq = ROPE_THETA ** (-jnp.arange(0, MLA_ROPE, 2, dtype=F32) / MLA_ROPE)
    freq_lane = jnp.concatenate([jnp.zeros((ROPE_LO,), F32), inv_freq, inv_freq,
                                 jnp.zeros((HEAD_PAD - ROPE_LO - 2 * half,), F32)]).reshape(1, HEAD_PAD)
    tables = _rope_tables(freq_lane, inv_freq.reshape(half, 1), seq, min(512, seq))

    x2d = x.reshape(t, D_MODEL)
    for l in range(depth):
        main, tail = _inproj(x2d, _row(pre_norm[l]), _pack_w_in(w_in[l]), tm)

        dsk = jnp.repeat(d_ssd[l].astype(F32), SSD_HEAD_DIM).reshape(1, SSD_D_INNER)
        y_a = _ssd(main, tail, conv_w[l].astype(F32), _row(conv_b[l]), _row(dt_bias[l], LANES),
                   _row(a_log[l], LANES), dsk, _row(ssd_norm[l]), bsz, seq)

        col3 = lambda a: a.astype(F32).reshape(S5_GROUPS, -1, 1)
        tile_l = lambda a: jnp.tile(a.astype(F32), (1, 1, S5_L))
        ls3 = s5_log_step[l].astype(F32).reshape(S5_GROUPS, 1, 1)
        toep, bin_t, cout = _s5_prep(
            ls3, col3(s5_lambda_re[l]), col3(s5_lambda_im[l]),
            tile_l(s5_b_re[l]), tile_l(s5_b_im[l]),
            tile_l(jnp.swapaxes(s5_c_re[l], 1, 2)), tile_l(jnp.swapaxes(s5_c_im[l], 1, 2)))
        u_tok = main[:, 3 * 1024:4 * 1024]
        u_g = (u_tok.reshape(bsz, nchunk, S5_L, S5_GROUPS, S5_GROUP)
               .transpose(3, 1, 0, 2, 4).reshape(S5_GROUPS, nchunk * bsz, S5_K))
        s_in = _s5_state(u_g, bin_t)
        row3 = lambda a: a.astype(F32).reshape(S5_GROUPS, 1, -1)
        xin = _s5_scan(ls3, row3(s5_lambda_re[l]), row3(s5_lambda_im[l]),
                       s_in.reshape(S5_GROUPS, nchunk, bsz, 2 * S5_STATE))
        y_g = _s5_out(u_g, toep, xin.reshape(S5_GROUPS, nchunk * bsz, 2 * S5_STATE), cout)
        y_tok = (y_g.reshape(S5_GROUPS, nchunk, bsz, S5_L, S5_GROUP)
                 .transpose(2, 1, 3, 0, 4).reshape(t, S5_WIDTH))
        y_b = _s5_post(y_tok, main, _row(s5_d[l]), w_glu[l].astype(BF16), _row(b_glu[l]), tm)

        stride_q = MLA_NOPE + MLA_ROPE
        wqt = _pad_heads(w_uq[l], 0, stride_q).T.astype(BF16)
        wk = _pad_heads(w_ukv[l], 0, MLA_NOPE).astype(BF16)
        wvt = (w_ukv[l].reshape(MLA_KV_RANK, MLA_HEADS, MLA_NOPE + MLA_V)[:, :, MLA_NOPE:]
               .reshape(MLA_KV_RANK, MLA_WIDTH).T.astype(BF16))
        q_t, k, v_t = _mla_prep(tail, _row(q_norm[l]), _row(kv_norm[l]), wqt, wk, wvt,
                                tables, seq, tm)
        o = _mla_attn(q_t, k, v_t, bsz, seq)

        x2d = _merge(x2d, y_a, y_b, o, main, w_a[l].astype(BF16), w_b[l].astype(BF16),
                     w_c[l].astype(BF16), w_o[l].astype(BF16), _row(post_norm[l]), min(512, seq))
    return x2d.reshape(bsz, seq, D_MODEL)
```

```python
import functools

import numpy as np

import jax
import jax.numpy as jnp
from jax import lax
from jax.experimental import pallas as pl
from jax.experimental.pallas import tpu as pltpu

F32 = jnp.float32
BF16 = jnp.bfloat16

D_MODEL = 1024
SSD_D_INNER = 1024
SSD_HEAD_DIM = 64
SSD_HEADS = 16
SSD_GROUPS = 4
SSD_STATE = 128
SSD_CONV = 4
SSD_CHUNK = 128
S5_WIDTH = 1024
S5_GROUP = 16
S5_GROUPS = 64
S5_STATE = 64
MLA_HEADS = 16
MLA_NOPE = 64
MLA_ROPE = 32
MLA_V = 64
MLA_Q_RANK = 512
MLA_KV_RANK = 256
MLA_WIDTH = 1024
ROPE_THETA = 10000.0
EPS = 1e-6
MASK_VALUE = -1e30

LANES = 128
BF16_ROWS = 16
S5_L = LANES
S5_K = S5_L * S5_GROUP
HEAD_PAD = 128
ROPE_LO = MLA_NOPE
VMEM_LIMIT = 56 * 1024 * 1024
HI = lax.Precision.HIGHEST
NT = (((1,), (1,)), ((), ()))


def _cparams(sem):
    return pltpu.CompilerParams(dimension_semantics=sem, vmem_limit_bytes=VMEM_LIMIT)


def _silu(x):
    return x * jax.nn.sigmoid(x)


def _rms(x, g):
    return x * lax.rsqrt(jnp.mean(x * x, axis=-1, keepdims=True) + EPS) * g


N_MAIN = 7
N_TRANS = 2
MAIN_ZC, MAIN_G0 = 3, 4
TAIL_CQ, TAIL_CKV, TAIL_SMALL = 0, 512, 768


def _inproj_kernel(x_ref, g_ref, w_ref, wt_ref, main_ref, tail_ref, ubt_ref, h_scr):
    j = pl.program_id(1)

    @pl.when(j == 0)
    def _():
        h_scr[...] = _rms(x_ref[...], g_ref[...]).astype(BF16)

    @pl.when(j < N_MAIN)
    def _():
        main_ref[...] = jnp.dot(h_scr[...], w_ref[...], preferred_element_type=F32).astype(BF16)

    @pl.when(j == N_MAIN)
    def _():
        tail_ref[...] = jnp.dot(h_scr[...], w_ref[...], preferred_element_type=F32)

    @pl.when(j > N_MAIN)
    def _():
        ubt_ref[...] = lax.dot_general(wt_ref[...], h_scr[...], NT,
                                       preferred_element_type=F32).astype(BF16)


def _inproj(x2d, gain, w_cat, w_t, tm):
    t = x2d.shape[0]
    tblk = lambda j: jnp.clip(j - (N_MAIN + 1), 0, N_TRANS - 1)
    return pl.pallas_call(
        _inproj_kernel,
        grid=(t // tm, N_MAIN + 1 + N_TRANS),
        in_specs=[
            pl.BlockSpec((tm, D_MODEL), lambda i, j: (i, 0)),
            pl.BlockSpec((1, D_MODEL), lambda i, j: (0, 0)),
            pl.BlockSpec((D_MODEL, 1024), lambda i, j: (0, jnp.minimum(j, N_MAIN))),
            pl.BlockSpec((1024, D_MODEL), lambda i, j: (tblk(j), 0)),
        ],
        out_specs=[
            pl.BlockSpec((tm, 1024), lambda i, j: (i, jnp.minimum(j, N_MAIN - 1))),
            pl.BlockSpec((tm, 1024), lambda i, j: (i, 0)),
            pl.BlockSpec((1024, tm), lambda i, j: (tblk(j), i)),
        ],
        out_shape=[
            jax.ShapeDtypeStruct((t, N_MAIN * 1024), BF16),
            jax.ShapeDtypeStruct((t, 1024), F32),
            jax.ShapeDtypeStruct((N_TRANS * 1024, t), BF16),
        ],
        scratch_shapes=[pltpu.VMEM((tm, D_MODEL), BF16)],
        compiler_params=_cparams(("parallel", "arbitrary")),
        name="inproj",
    )(x2d, gain, w_cat, w_t)


SSD_PAD = BF16_ROWS
HEADS_PER_GROUP = SSD_HEADS // SSD_GROUPS


def _ssd_constants():
    L = SSD_CHUNK
    shift = np.zeros((SSD_CONV * L, SSD_PAD + L), np.float32)
    for k in range(SSD_CONV):
        for t in range(L):
            shift[k * L + t, SSD_PAD + t - (SSD_CONV - 1) + k] = 1.0
    e_wide = np.zeros((3 * LANES, SSD_HEADS * LANES), np.float32)
    e_head = np.zeros((3 * LANES, SSD_HEADS * SSD_HEAD_DIM), np.float32)
    for part in range(3):
        for h in range(SSD_HEADS):
            e_wide[part * LANES + h, h * LANES:(h + 1) * LANES] = 1.0
            e_head[part * LANES + h, h * SSD_HEAD_DIM:(h + 1) * SSD_HEAD_DIM] = 1.0
    return (jnp.asarray(shift, BF16), jnp.asarray(e_wide, BF16), jnp.asarray(e_head, BF16))


def _split3(x):
    hi = x.astype(BF16)
    r1 = x - hi.astype(F32)
    mid = r1.astype(BF16)
    lo = (r1 - mid.astype(F32)).astype(BF16)
    return jnp.concatenate([hi, mid, lo], axis=-1)


def _ssd_kernel(za_ref, xs_ref, bc_ref, sm_ref, cw_ref, cb_ref, dtb_ref, alog_ref,
                dsk_ref, ng_ref, shift_ref, ewide_ref, ehead_ref, y_ref, xext, state):
    L = SSD_CHUNK
    c = pl.program_id(1)

    @pl.when(c == 0)
    def _():
        xext[0:SSD_PAD, :] = jnp.zeros((SSD_PAD, 2048), BF16)
        state[...] = jnp.zeros(state.shape, F32)

    xext[SSD_PAD:SSD_PAD + L, 0:1024] = xs_ref[...]
    xext[SSD_PAD:SSD_PAD + L, 1024:2048] = bc_ref[...]
    shifted = jnp.dot(shift_ref[...], xext[...], preferred_element_type=F32)
    acc = cb_ref[...]
    for k in range(SSD_CONV):
        acc = acc + cw_ref[k:k + 1, :] * shifted[k * L:(k + 1) * L, :]
    xbc = _silu(acc)
    xext[0:SSD_PAD, :] = xext[L:L + SSD_PAD, :]

    lane = lax.broadcasted_iota(jnp.int32, (1, LANES), 1)
    dt = jax.nn.softplus(sm_ref[...] + dtb_ref[...])
    a = jnp.where(lane < SSD_HEADS, -jnp.exp(alog_ref[...]), 0.0)
    adt = dt * a
    row_i = lax.broadcasted_iota(jnp.int32, (L, L), 0)
    col_i = lax.broadcasted_iota(jnp.int32, (L, L), 1)
    causal = row_i >= col_i
    cs = jnp.dot(causal.astype(F32), adt, preferred_element_type=F32, precision=HI)
    cs_t = cs.T
    dt_t = dt.T
    cs3 = _split3(cs)
    cs_wide = jnp.dot(cs3, ewide_ref[...], preferred_element_type=F32)
    cs_head = jnp.dot(cs3, ehead_ref[...], preferred_element_type=F32)

    xs_all = xbc[:, 0:SSD_D_INNER]
    xs16 = xs_all.astype(BF16)
    pair_lane = lax.broadcasted_iota(jnp.int32, (1, 2 * SSD_HEAD_DIM), 1) < SSD_HEAD_DIM
    y_groups = []
    for g in range(SSD_GROUPS):
        b_g = xbc[:, SSD_D_INNER + g * SSD_STATE:SSD_D_INNER + (g + 1) * SSD_STATE]
        c_g = xbc[:, SSD_D_INNER + (SSD_GROUPS + g) * SSD_STATE:
                  SSD_D_INNER + (SSD_GROUPS + g + 1) * SSD_STATE]
        c_g16 = c_g.astype(BF16)
        b_gt = b_g.T
        cb = jnp.dot(c_g16, b_gt.astype(BF16), preferred_element_type=F32)
        st = state[g]
        y_off = jnp.dot(c_g16, st.astype(BF16), preferred_element_type=F32)
        y_pairs, st_pairs = [], []
        for pr in range(HEADS_PER_GROUP // 2):
            lhs = []
            for r in range(2):
                h = g * HEADS_PER_GROUP + 2 * pr + r
                row = cs_t[h:h + 1, :]
                last = cs_t[h:h + 1, L - 1:L]
                dt_row = dt_t[h:h + 1, :]
                decay = jnp.exp(jnp.where(causal, cs_wide[:, h * LANES:(h + 1) * LANES] - row, MASK_VALUE))
                lhs.append((cb * decay * dt_row).astype(BF16))
                lhs.append((b_gt * (jnp.exp(last - row) * dt_row)).astype(BF16))
            h0 = g * HEADS_PER_GROUP + 2 * pr
            res = jnp.dot(jnp.concatenate(lhs, axis=0), xs16[:, h0 * SSD_HEAD_DIM:(h0 + 2) * SSD_HEAD_DIM],
                          preferred_element_type=F32)
            y_pairs.append(jnp.where(pair_lane, res[0:L], res[2 * L:3 * L]))
            st_pairs.append(jnp.where(pair_lane, res[L:2 * L], res[3 * L:4 * L]))
        hsl = slice(g * HEADS_PER_GROUP * SSD_HEAD_DIM, (g + 1) * HEADS_PER_GROUP * SSD_HEAD_DIM)
        cs_g = cs_head[:, hsl]
        y_groups.append(jnp.concatenate(y_pairs, axis=-1) + jnp.exp(cs_g) * y_off)
        state[g] = st * jnp.exp(cs_g[L - 1:L, :]) + jnp.concatenate(st_pairs, axis=-1)

    y = jnp.concatenate(y_groups, axis=-1) + dsk_ref[...] * xs_all
    y = y * _silu(za_ref[...].astype(F32))
    y_ref[...] = _rms(y, ng_ref[...]).astype(BF16)


def _ssd(main, tail, conv_w, conv_b, dtb, alog, dsk, ng, bsz, seq):
    nc = seq // SSD_CHUNK
    L = SSD_CHUNK
    shift, e_wide, e_head = _ssd_constants()
    row = lambda b, c: b * nc + c
    vec = lambda n: pl.BlockSpec((1, n), lambda b, c: (0, 0))
    full = lambda a: pl.BlockSpec(a.shape, lambda b, c: (0, 0))
    return pl.pallas_call(
        _ssd_kernel,
        grid=(bsz, nc),
        in_specs=[
            pl.BlockSpec((L, 1024), lambda b, c: (row(b, c), 0)),
            pl.BlockSpec((L, 1024), lambda b, c: (row(b, c), 1)),
            pl.BlockSpec((L, 1024), lambda b, c: (row(b, c), 2)),
            pl.BlockSpec((L, LANES), lambda b, c: (row(b, c), TAIL_SMALL // LANES)),
            pl.BlockSpec((SSD_CONV, 2048), lambda b, c: (0, 0)),
            vec(2048), vec(LANES), vec(LANES), vec(1024), vec(1024),
            full(shift), full(e_wide), full(e_head),
        ],
        out_specs=pl.BlockSpec((L, 1024), lambda b, c: (row(b, c), 0)),
        out_shape=jax.ShapeDtypeStruct((bsz * seq, SSD_D_INNER), BF16),
        scratch_shapes=[
            pltpu.VMEM((SSD_PAD + L, 2048), BF16),
            pltpu.VMEM((SSD_GROUPS, SSD_STATE, HEADS_PER_GROUP * SSD_HEAD_DIM), F32),
        ],
        compiler_params=_cparams(("parallel", "arbitrary")),
        name="ssd",
    )(main, main, main, tail, conv_w, conv_b, dtb, alog, dsk, ng, shift, e_wide, e_head)


def _s5_pow(step, lr, li, k):
    mag = jnp.exp(lr * step * k)
    ang = li * step * k
    return mag * jnp.cos(ang), mag * jnp.sin(ang)


def _s5_kernel(ls_ref, lrow_ref, lcol_ref, bt_ref, cn_ref, ct_ref, u_ref, y_ref,
               k_scr, toep, bin_scr, cout, s_scr, xin_scr, *, bsz):
    P, H, L = S5_STATE, S5_GROUP, S5_L
    rows = u_ref.shape[1]
    nchunk = rows // bsz
    lane = lax.broadcasted_iota(jnp.int32, (1, 2 * P), 1)
    lo_lane = lane < P
    sub = lax.broadcasted_iota(jnp.int32, (2 * P, 1), 0)
    lo_sub = sub < P

    step = jnp.exp(ls_ref[0])
    lr, li = lrow_ref[0, 0:1, :], lrow_ref[0, 1:2, :]
    lr_c, li_c = lcol_ref[0, :, 0:1], lcol_ref[0, :, 1:2]

    abar_re, abar_im = _s5_pow(step, lr, li, 1.0)
    den = lr * lr + li * li
    nr, ni = abar_re - 1.0, abar_im
    f_re = (nr * lr + ni * li) / den
    f_im = (ni * lr - nr * li) / den
    br, bi = bt_ref[0, 0], bt_ref[0, 1]
    bb_re = f_re * br - f_im * bi
    bb_im = f_re * bi + f_im * br
    v1 = jnp.where(lo_lane, bb_re, bb_im)
    v2 = jnp.where(lo_lane, -bb_im, bb_re)

    cr, ci = cn_ref[0, 0], cn_ref[0, 1]
    g_rows = []
    for hp in range(H):
        g_re = bb_re[hp:hp + 1, :] * cr - bb_im[hp:hp + 1, :] * ci
        g_im = bb_re[hp:hp + 1, :] * ci + bb_im[hp:hp + 1, :] * cr
        g_rows.append(jnp.where(lo_lane, g_re, -g_im))
    g_mat = jnp.concatenate(g_rows, axis=0)
    lag = lax.broadcasted_iota(jnp.int32, (1, L), 1).astype(F32)
    p_re, p_im = _s5_pow(step, lr_c, li_c, lag)
    k_scr[...] = jnp.dot(g_mat, jnp.where(lo_sub, p_re, p_im), precision=HI,
                         preferred_element_type=F32)

    srow = lax.broadcasted_iota(jnp.int32, (L, L), 0)
    tcol = lax.broadcasted_iota(jnp.int32, (L, L), 1)
    keep = tcol >= srow

    def toep_rows(hp, carry):
        roff = pl.multiple_of(hp * L, L)
        koff = pl.multiple_of(hp * H, H)
        for h in range(H):
            kv = jnp.broadcast_to(k_scr[pl.ds(koff + h, 1), :], (L, L))
            blk = pltpu.roll(kv, 0, 1, stride=1, stride_axis=0)
            toep[pl.ds(roff, L), h * L:(h + 1) * L] = jnp.where(keep, blk, 0.0).astype(BF16)
        return carry

    lax.fori_loop(0, H, toep_rows, 0)

    s_col = lax.broadcasted_iota(jnp.int32, (L, 1), 0).astype(F32)
    q_re, q_im = _s5_pow(step, lr, li, (L - 1.0) - s_col)
    for hp in range(H):
        bin_scr[hp * L:(hp + 1) * L, :] = (q_re * v1[hp:hp + 1, :] + q_im * v2[hp:hp + 1, :]).astype(BF16)

    r_re, r_im = _s5_pow(step, lr_c, li_c, lag + 1.0)
    crt, cit = ct_ref[0, 0], ct_ref[0, 1]
    for h in range(H):
        c_r, c_i = crt[:, h:h + 1], cit[:, h:h + 1]
        cout[:, h * L:(h + 1) * L] = jnp.where(lo_sub, c_r * r_re - c_i * r_im,
                                               -(c_r * r_im + c_i * r_re)).astype(BF16)

    u = jnp.concatenate([u_ref[hp] for hp in range(H)], axis=-1)
    s_scr[...] = jnp.dot(u, bin_scr[...], preferred_element_type=F32)

    al_re, al_im = _s5_pow(step, lr, li, float(L))
    a1 = jnp.broadcast_to(al_re, (bsz, 2 * P))
    a2 = jnp.broadcast_to(jnp.where(lo_lane, -al_im, al_im), (bsz, 2 * P))

    x = xs = jnp.zeros((bsz, 2 * P), F32)
    for cidx in range(nchunk):
        sl = pl.ds(cidx, bsz, stride=nchunk)
        xin_scr[sl, :] = x
        s = s_scr[sl, :]
        x, xs = a1 * x + a2 * xs + s, a1 * xs - a2 * x + pltpu.roll(s, P, 1)

    y = jnp.dot(u, toep[...], preferred_element_type=F32)
    y = y + jnp.dot(xin_scr[...].astype(BF16), cout[...], preferred_element_type=F32)
    for h in range(H):
        y_ref[h] = y[:, h * L:(h + 1) * L].astype(BF16)


def _s5(ubt, ls, lrow, lcol, bt, cn, ct, bsz, seq):
    t = ubt.shape[1]
    rows = t // S5_L
    u3 = ubt.reshape(ubt.shape[0], rows, S5_L)
    P, H = S5_STATE, S5_GROUP
    g4 = lambda a, b: pl.BlockSpec((1, 2, a, b), lambda g: (g, 0, 0, 0))
    y3 = pl.pallas_call(
        functools.partial(_s5_kernel, bsz=bsz),
        grid=(S5_GROUPS,),
        in_specs=[pl.BlockSpec((1, 1, 1), lambda g: (g, 0, 0)),
                  pl.BlockSpec((1, 2, 2 * P), lambda g: (g, 0, 0)),
                  pl.BlockSpec((1, 2 * P, 2), lambda g: (g, 0, 0)),
                  g4(H, 2 * P), g4(H, 2 * P), g4(2 * P, H),
                  pl.BlockSpec((H, rows, S5_L), lambda g: (g, 0, 0))],
        out_specs=pl.BlockSpec((H, rows, S5_L), lambda g: (g, 0, 0)),
        out_shape=jax.ShapeDtypeStruct((S5_WIDTH, rows, S5_L), BF16),
        scratch_shapes=[pltpu.VMEM((H * H, S5_L), F32),
                        pltpu.VMEM((S5_K, S5_K), BF16),
                        pltpu.VMEM((S5_K, 2 * P), BF16),
                        pltpu.VMEM((2 * P, S5_K), BF16),
                        pltpu.VMEM((rows, 2 * P), F32),
                        pltpu.VMEM((rows, 2 * P), F32)],
        compiler_params=_cparams(("parallel",)),
        name="s5",
    )(ls, lrow, lcol, bt, cn, ct, u3)
    return y3.reshape(S5_WIDTH, t)


def _gelu_tanh(x):
    return 0.5 * x * (1.0 + jnp.tanh(0.7978845608028654 * (x + 0.044715 * (x * x * x))))


def _s5_post_kernel(y_ref, u_ref, z_ref, d_ref, wt_ref, b_ref, o_ref):
    y = y_ref[...].astype(F32) + d_ref[...] * u_ref[...].astype(F32)
    gl = _gelu_tanh(y)
    gate = jnp.dot(wt_ref[...], gl.astype(BF16), preferred_element_type=F32) + b_ref[...]
    out = gl * jax.nn.sigmoid(gate) * _silu(z_ref[...].astype(F32))
    o_ref[...] = out.T.astype(BF16)


def _s5_post(y_t, ubt, d_col, w_glu_t, b_col, tn):
    t = y_t.shape[1]
    colv = pl.BlockSpec((1024, 1), lambda i: (0, 0))
    return pl.pallas_call(
        _s5_post_kernel,
        grid=(t // tn,),
        in_specs=[pl.BlockSpec((1024, tn), lambda i: (0, i)),
                  pl.BlockSpec((1024, tn), lambda i: (0, i)),
                  pl.BlockSpec((1024, tn), lambda i: (1, i)),
                  colv,
                  pl.BlockSpec((1024, 1024), lambda i: (0, 0)),
                  colv],
        out_specs=pl.BlockSpec((tn, 1024), lambda i: (i, 0)),
        out_shape=jax.ShapeDtypeStruct((t, S5_WIDTH), BF16),
        compiler_params=_cparams(("parallel",)),
        name="s5_post",
    )(y_t, ubt, ubt, d_col, w_glu_t, b_col)


def _rope_tables_kernel(f_ref, fcol_ref, cos_ref, sa_ref, sb_ref, cost_ref, sint_ref):
    ts = cos_ref.shape[0]
    base = pl.program_id(0) * ts
    pos = (base + lax.broadcasted_iota(jnp.int32, (ts, 1), 0)).astype(F32)
    ang = pos * f_ref[...]
    cos, sin = jnp.cos(ang), jnp.sin(ang)
    lane = lax.broadcasted_iota(jnp.int32, (1, HEAD_PAD), 1)
    half = MLA_ROPE // 2
    lo = (lane >= ROPE_LO) & (lane < ROPE_LO + half)
    hi = (lane >= ROPE_LO + half) & (lane < ROPE_LO + MLA_ROPE)
    cos_ref[...] = jnp.where(lo | hi, cos, 0.0)
    sa_ref[...] = jnp.where(lo, -sin, 0.0)
    sb_ref[...] = jnp.where(hi, sin, 0.0)
    pos_t = (base + lax.broadcasted_iota(jnp.int32, (1, ts), 1)).astype(F32)
    ang_t = fcol_ref[...] * pos_t
    cost_ref[...] = jnp.cos(ang_t)
    sint_ref[...] = jnp.sin(ang_t)


def _rope_tables(freq_lane, freq_col, seq, ts):
    half = MLA_ROPE // 2
    out = jax.ShapeDtypeStruct((seq, HEAD_PAD), F32)
    out_t = jax.ShapeDtypeStruct((half, seq), F32)
    blk = pl.BlockSpec((ts, HEAD_PAD), lambda i: (i, 0))
    blk_t = pl.BlockSpec((half, ts), lambda i: (0, i))
    return pl.pallas_call(
        _rope_tables_kernel,
        grid=(seq // ts,),
        in_specs=[pl.BlockSpec((1, HEAD_PAD), lambda i: (0, 0)),
                  pl.BlockSpec((half, 1), lambda i: (0, 0))],
        out_specs=[blk, blk, blk, blk_t, blk_t],
        out_shape=[out, out, out, out_t, out_t],
        compiler_params=_cparams(("parallel",)),
        name="rope_tables",
    )(freq_lane, freq_col)


def _rope(x, cos, sa, sb):
    half = MLA_ROPE // 2
    return (x * cos + pltpu.roll(x, HEAD_PAD - half, 1) * sa + pltpu.roll(x, half, 1) * sb)


ATT_T = 512
ATT_HP = 2
V_ROWS = MLA_V + BF16_ROWS
LOG2E = 1.4426950408889634


def _mla_prep_kernel(cq_ref, ckv_ref, sm_ref, qn_ref, kvn_ref, wqt_ref, wk_ref, wvt_ref,
                     cos_ref, sa_ref, sb_ref, cost_ref, sint_ref, qt_ref, k_ref, vt_ref):
    tm = cq_ref.shape[0]
    half = MLA_ROPE // 2
    scale = (MLA_NOPE + MLA_ROPE) ** -0.5 * LOG2E
    qn = _rms(cq_ref[...], qn_ref[...]).astype(BF16)
    q_t = lax.dot_general(wqt_ref[...], qn, NT, preferred_element_type=F32)
    c_t, s_t = cost_ref[...], sint_ref[...]
    for h in range(MLA_HEADS):
        r0 = h * HEAD_PAD
        r1 = r0 + ROPE_LO
        qt_ref[r0:r1, :] = (q_t[r0:r1, :] * scale).astype(BF16)
        x1, x2 = q_t[r1:r1 + half, :], q_t[r1 + half:r1 + 2 * half, :]
        qt_ref[r1:r1 + half, :] = ((x1 * c_t - x2 * s_t) * scale).astype(BF16)
        qt_ref[r1 + half:r1 + 2 * half, :] = ((x1 * s_t + x2 * c_t) * scale).astype(BF16)
        qt_ref[r1 + 2 * half:r0 + HEAD_PAD, :] = jnp.zeros((HEAD_PAD - ROPE_LO - 2 * half, tm), BF16)

    kvn = _rms(ckv_ref[...], kvn_ref[...]).astype(BF16)
    k = jnp.dot(kvn, wk_ref[...], preferred_element_type=F32)
    kr = _rope(sm_ref[...], cos_ref[...], sa_ref[...], sb_ref[...])
    for h in range(MLA_HEADS):
        sl = slice(h * HEAD_PAD, (h + 1) * HEAD_PAD)
        k_ref[:, sl] = (k[:, sl] + kr).astype(BF16)
    v_t = lax.dot_general(wvt_ref[...], kvn, NT, preferred_element_type=F32)
    ones_row = (lax.broadcasted_iota(jnp.int32, (BF16_ROWS, ATT_T), 0) == 0).astype(BF16)
    for j in range(tm // ATT_T):
        for h in range(MLA_HEADS):
            vt_ref[j, h * V_ROWS:h * V_ROWS + MLA_V, :] = (
                v_t[h * MLA_V:(h + 1) * MLA_V, j * ATT_T:(j + 1) * ATT_T].astype(BF16))
            vt_ref[j, h * V_ROWS + MLA_V:(h + 1) * V_ROWS, :] = ones_row


def _mla_prep(tail, qn, kvn, wqt, wk, wvt, tables, seq, tm):
    cos, sa, sb, cost, sint = tables
    t = tail.shape[0]
    npos = seq // tm
    half = MLA_ROPE // 2
    tab = pl.BlockSpec((tm, HEAD_PAD), lambda i: (i % npos, 0))
    tab_t = pl.BlockSpec((half, tm), lambda i: (0, i % npos))
    full = lambda a, b: pl.BlockSpec((a, b), lambda i: (0, 0))
    hw = MLA_HEADS * HEAD_PAD
    nkv = tm // ATT_T
    return pl.pallas_call(
        _mla_prep_kernel,
        grid=(t // tm,),
        in_specs=[pl.BlockSpec((tm, MLA_Q_RANK), lambda i: (i, TAIL_CQ // MLA_Q_RANK)),
                  pl.BlockSpec((tm, MLA_KV_RANK), lambda i: (i, TAIL_CKV // MLA_KV_RANK)),
                  pl.BlockSpec((tm, LANES), lambda i: (i, TAIL_SMALL // LANES)),
                  full(1, MLA_Q_RANK), full(1, MLA_KV_RANK),
                  full(hw, MLA_Q_RANK), full(MLA_KV_RANK, hw), full(MLA_WIDTH, MLA_KV_RANK),
                  tab, tab, tab, tab_t, tab_t],
        out_specs=[pl.BlockSpec((hw, tm), lambda i: (0, i)),
                   pl.BlockSpec((tm, hw), lambda i: (i, 0)),
                   pl.BlockSpec((nkv, MLA_HEADS * V_ROWS, ATT_T), lambda i: (i, 0, 0))],
        out_shape=[jax.ShapeDtypeStruct((hw, t), BF16),
                   jax.ShapeDtypeStruct((t, hw), BF16),
                   jax.ShapeDtypeStruct((t // ATT_T, MLA_HEADS * V_ROWS, ATT_T), BF16)],
        compiler_params=_cparams(("parallel",)),
        name="mla_prep",
    )(tail, tail, tail, qn, kvn, wqt, wk, wvt, cos, sa, sb, cost, sint)


def _attn_kernel(qt_ref, k_ref, vt_ref, o_ref, acc_scr, m_scr, s0, s1, mb0, mb1):
    t = ATT_T
    qi = pl.program_id(2)
    m_scr[...] = jnp.full(m_scr.shape, MASK_VALUE, F32)
    acc_scr[...] = jnp.zeros(acc_scr.shape, F32)

    def scores(kidx, s_buf, mb_buf, masked):
        off = pl.multiple_of(kidx * t, t)
        for hh in range(ATT_HP):
            q_t = qt_ref[hh * HEAD_PAD:(hh + 1) * HEAD_PAD, :]
            k = k_ref[pl.ds(off, t), hh * HEAD_PAD:(hh + 1) * HEAD_PAD]
            s = jnp.dot(k, q_t, preferred_element_type=F32)
            if masked:
                kv_i = lax.broadcasted_iota(jnp.int32, (t, t), 0)
                q_i = lax.broadcasted_iota(jnp.int32, (t, t), 1)
                s = jnp.where(kv_i <= q_i, s, MASK_VALUE)
            s_buf[hh] = s
            mb_buf[hh:hh + 1, :] = jnp.max(s, axis=0, keepdims=True)

    def update(vidx, s_buf, mb_buf):
        for hh in range(ATT_HP):
            rows = slice(hh * V_ROWS, (hh + 1) * V_ROWS)
            m_old = m_scr[hh:hh + 1, :]
            m_new = jnp.maximum(m_old, mb_buf[hh:hh + 1, :])
            alpha = jnp.exp2(m_old - m_new)
            p = jnp.exp2(s_buf[hh] - m_new).astype(BF16)
            m_scr[hh:hh + 1, :] = m_new
            pv = jnp.dot(vt_ref[vidx, rows, :], p, preferred_element_type=F32)
            acc_scr[rows, :] = alpha * acc_scr[rows, :] + pv

    block_at = lambda pos: jnp.where(pos == 0, qi, pos - 1)
    scores(qi, s0, mb0, True)

    def pair(j, carry):
        scores(2 * j, s1, mb1, False)
        update(block_at(2 * j), s0, mb0)
        scores(2 * j + 1, s0, mb0, False)
        update(2 * j, s1, mb1)
        return carry

    lax.fori_loop(0, qi // 2, pair, 0)

    @pl.when(qi % 2 == 1)
    def _():
        scores(qi - 1, s1, mb1, False)
        update(block_at(qi - 1), s0, mb0)
        update(qi - 1, s1, mb1)

    @pl.when(qi % 2 == 0)
    def _():
        update(block_at(qi), s0, mb0)

    outs = []
    for hh in range(ATT_HP):
        r0 = hh * V_ROWS
        outs.append(acc_scr[r0:r0 + MLA_V, :] / acc_scr[r0 + MLA_V:r0 + MLA_V + 1, :])
    o_ref[...] = jnp.concatenate(outs, axis=0).T.astype(BF16)


def _mla_attn(q_t, k, v_t, bsz, seq):
    nq = seq // ATT_T
    nhp = MLA_HEADS // ATT_HP
    return pl.pallas_call(
        _attn_kernel,
        grid=(bsz, nhp, nq),
        in_specs=[pl.BlockSpec((ATT_HP * HEAD_PAD, ATT_T), lambda b, h, i: (h, b * nq + i)),
                  pl.BlockSpec((seq, ATT_HP * HEAD_PAD), lambda b, h, i: (b, h)),
                  pl.BlockSpec((nq, ATT_HP * V_ROWS, ATT_T), lambda b, h, i: (b, h, 0))],
        out_specs=pl.BlockSpec((ATT_T, ATT_HP * MLA_V), lambda b, h, i: (b * nq + i, h)),
        out_shape=jax.ShapeDtypeStruct((bsz * seq, MLA_WIDTH), BF16),
        scratch_shapes=[pltpu.VMEM((ATT_HP * V_ROWS, ATT_T), F32),
                        pltpu.VMEM((8, ATT_T), F32),
                        pltpu.VMEM((ATT_HP, ATT_T, ATT_T), F32),
                        pltpu.VMEM((ATT_HP, ATT_T, ATT_T), F32),
                        pltpu.VMEM((8, ATT_T), F32),
                        pltpu.VMEM((8, ATT_T), F32)],
        compiler_params=_cparams(("parallel", "parallel", "arbitrary")),
        name="mla_attn",
    )(q_t, k, v_t)


def _merge_kernel(x_ref, ya_ref, yb_ref, o_ref, zc_ref, g0_ref, g1_ref, g2_ref,
                  wa_ref, wb_ref, wc_ref, wo_ref, pn_ref, out_ref):
    dot = lambda a, w: jnp.dot(a, w[...], preferred_element_type=F32)
    sig = lambda r: jax.nn.sigmoid(r[...].astype(F32))
    yc = (o_ref[...].astype(F32) * _silu(zc_ref[...].astype(F32))).astype(BF16)
    merged = (sig(g0_ref) * dot(ya_ref[...], wa_ref)
              + sig(g1_ref) * dot(yb_ref[...], wb_ref)
              + sig(g2_ref) * dot(yc, wc_ref))
    out = dot(merged.astype(BF16), wo_ref)
    out_ref[...] = x_ref[...] + _rms(out, pn_ref[...])


def _merge(x2d, ya, yb, o, main, wa, wb, wc, wo, pn, tm):
    t = x2d.shape[0]
    col = lambda j: pl.BlockSpec((tm, 1024), lambda i: (i, j))
    wfull = pl.BlockSpec((1024, 1024), lambda i: (0, 0))
    return pl.pallas_call(
        _merge_kernel,
        grid=(t // tm,),
        in_specs=[col(0), col(0), col(0), col(0), col(MAIN_ZC), col(MAIN_G0), col(MAIN_G0 + 1),
                  col(MAIN_G0 + 2), wfull, wfull, wfull, wfull,
                  pl.BlockSpec((1, 1024), lambda i: (0, 0))],
        out_specs=col(0),
        out_shape=jax.ShapeDtypeStruct((t, D_MODEL), F32),
        compiler_params=_cparams(("parallel",)),
        name="merge",
    )(x2d, ya, yb, o, main, main, main, main, wa, wb, wc, wo, pn)


def _pack_w_in(w):
    sizes = (SSD_D_INNER, SSD_D_INNER + 2 * SSD_GROUPS * SSD_STATE, SSD_HEADS, S5_WIDTH, S5_WIDTH,
             MLA_Q_RANK, MLA_KV_RANK, MLA_ROPE, MLA_WIDTH, 3 * D_MODEL)
    parts, start = [], 0
    for n in sizes:
        parts.append(w[:, start:start + n])
        start += n
    z_a, xbc, dt, u_b, z_b, c_q, c_kv, k_rope, z_c, gates = parts
    zeros = lambda n: jnp.zeros((w.shape[0], n), w.dtype)
    small = jnp.concatenate([dt, zeros(ROPE_LO - SSD_HEADS), k_rope,
                             zeros(LANES - ROPE_LO - MLA_ROPE)], axis=1)
    tail = jnp.concatenate([c_q, c_kv, small, zeros(1024 - TAIL_SMALL - LANES)], axis=1)
    w_cat = jnp.concatenate([z_a, xbc, z_c, gates, tail], axis=1).astype(BF16)
    w_t = jnp.concatenate([u_b, z_b], axis=1).T.astype(BF16)
    return w_cat, w_t


def _pad_heads(w, lo, width):
    r = w.shape[0]
    w3 = w.reshape(r, MLA_HEADS, -1)[:, :, lo:lo + width]
    w3 = jnp.pad(w3, ((0, 0), (0, 0), (0, HEAD_PAD - width)))
    return w3.reshape(r, MLA_HEADS * HEAD_PAD)


def _row(v, n=None):
    v = v.reshape(1, -1).astype(F32)
    if n is not None and v.shape[1] < n:
        v = jnp.pad(v, ((0, 0), (0, n - v.shape[1])))
    return v


def _dup(a, axis):
    return jnp.concatenate([a, a], axis=axis).astype(F32)


def kernel(x, pre_norm, w_in, conv_w, conv_b, dt_bias, a_log, d_ssd, ssd_norm, w_a, s5_log_step, s5_lambda_re, s5_lambda_im, s5_b_re, s5_b_im, s5_c_re, s5_c_im, s5_d, w_glu, b_glu, w_b, q_norm, w_uq, kv_norm, w_ukv, w_c, w_o, post_norm):
    bsz, seq, _ = x.shape
    depth = w_in.shape[0]
    t = bsz * seq
    assert seq % ATT_T == 0 and seq % SSD_CHUNK == 0 and seq % S5_L == 0
    tm = min(1024, seq)

    half = MLA_ROPE // 2
    inv_freq = ROPE_THETA ** (-jnp.arange(0, MLA_ROPE, 2, dtype=F32) / MLA_ROPE)
    freq_lane = jnp.concatenate([jnp.zeros((ROPE_LO,), F32), inv_freq, inv_freq,
                                 jnp.zeros((HEAD_PAD - ROPE_LO - 2 * half,), F32)]).reshape(1, HEAD_PAD)
    tables = _rope_tables(freq_lane, inv_freq.reshape(half, 1), seq, min(512, seq))

    x2d = x.reshape(t, D_MODEL)
    for l in range(depth):
        w_cat, w_t = _pack_w_in(w_in[l])
        main, tail, ubt = _inproj(x2d, _row(pre_norm[l]), w_cat, w_t, tm)

        dsk = jnp.repeat(d_ssd[l].astype(F32), SSD_HEAD_DIM).reshape(1, SSD_D_INNER)
        y_a = _ssd(main, tail, conv_w[l].astype(F32), _row(conv_b[l]), _row(dt_bias[l], LANES),
                   _row(a_log[l], LANES), dsk, _row(ssd_norm[l]), bsz, seq)

        ls3 = s5_log_step[l].astype(F32).reshape(S5_GROUPS, 1, 1)
        lam = jnp.stack([s5_lambda_re[l], s5_lambda_im[l]], axis=1)
        lrow = _dup(lam, 2)
        lcol = jnp.swapaxes(lrow, 1, 2)
        bt = _dup(jnp.stack([jnp.swapaxes(s5_b_re[l], 1, 2),
                             jnp.swapaxes(s5_b_im[l], 1, 2)], axis=1), 3)
        cn = _dup(jnp.stack([s5_c_re[l], s5_c_im[l]], axis=1), 3)
        ct = jnp.swapaxes(cn, 2, 3)
        y_t = _s5(ubt, ls3, lrow, lcol, bt, cn, ct, bsz, seq)
        y_b = _s5_post(y_t, ubt, s5_d[l].astype(F32).reshape(S5_WIDTH, 1), w_glu[l].T.astype(BF16),
                       b_glu[l].astype(F32).reshape(S5_WIDTH, 1), min(512, seq))

        stride_q = MLA_NOPE + MLA_ROPE
        wqt = _pad_heads(w_uq[l], 0, stride_q).T.astype(BF16)
        wk = _pad_heads(w_ukv[l], 0, MLA_NOPE).astype(BF16)
        wvt = (w_ukv[l].reshape(MLA_KV_RANK, MLA_HEADS, MLA_NOPE + MLA_V)[:, :, MLA_NOPE:]
               .reshape(MLA_KV_RANK, MLA_WIDTH).T.astype(BF16))
        q_t, k, v_t = _mla_prep(tail, _row(q_norm[l]), _row(kv_norm[l]), wqt, wk, wvt,
                                tables, seq, tm)
        o = _mla_attn(q_t, k, v_t, bsz, seq)

        x2d = _merge(x2d, y_a, y_b, o, main, w_a[l].astype(BF16), w_b[l].astype(BF16),
                     w_c[l].astype(BF16), w_o[l].astype(BF16), _row(post_norm[l]), min(512, seq))
    return x2d.reshape(bsz, seq, D_MODEL)
```

```python
import functools

import numpy as np

import jax
import jax.numpy as jnp
from jax import lax
from jax.experimental import pallas as pl
from jax.experimental.pallas import tpu as pltpu

F32 = jnp.float32
BF16 = jnp.bfloat16

D_MODEL = 1024
SSD_D_INNER = 1024
SSD_HEAD_DIM = 64
SSD_HEADS = 16
SSD_GROUPS = 4
SSD_STATE = 128
SSD_CONV = 4
SSD_CHUNK = 128
S5_WIDTH = 1024
S5_GROUP = 16
S5_GROUPS = 64
S5_STATE = 64
MLA_HEADS = 16
MLA_NOPE = 64
MLA_ROPE = 32
MLA_V = 64
MLA_Q_RANK = 512
MLA_KV_RANK = 256
MLA_WIDTH = 1024
ROPE_THETA = 10000.0
EPS = 1e-6
MASK_VALUE = -1e30

LANES = 128
BF16_ROWS = 16
S5_L = LANES
S5_K = S5_L * S5_GROUP
HEAD_PAD = 128
ROPE_LO = MLA_NOPE
VMEM_LIMIT = 56 * 1024 * 1024
HI = lax.Precision.HIGHEST
NT = (((1,), (1,)), ((), ()))


def _cparams(sem):
    return pltpu.CompilerParams(dimension_semantics=sem, vmem_limit_bytes=VMEM_LIMIT)


def _silu(x):
    return x * jax.nn.sigmoid(x)


def _rms(x, g):
    return x * lax.rsqrt(jnp.mean(x * x, axis=-1, keepdims=True) + EPS) * g


N_MAIN = 7
N_TRANS = 2
MAIN_ZC, MAIN_G0 = 3, 4
TAIL_CQ, TAIL_CKV, TAIL_SMALL = 0, 512, 768


def _inproj_kernel(x_ref, g_ref, w_ref, wt_ref, main_ref, tail_ref, ubt_ref, h_scr):
    j = pl.program_id(1)

    @pl.when(j == 0)
    def _():
        h_scr[...] = _rms(x_ref[...], g_ref[...]).astype(BF16)

    @pl.when(j < N_MAIN)
    def _():
        main_ref[...] = jnp.dot(h_scr[...], w_ref[...], preferred_element_type=F32).astype(BF16)

    @pl.when(j == N_MAIN)
    def _():
        tail_ref[...] = jnp.dot(h_scr[...], w_ref[...], preferred_element_type=F32)

    @pl.when(j > N_MAIN)
    def _():
        ubt_ref[...] = lax.dot_general(wt_ref[...], h_scr[...], NT,
                                       preferred_element_type=F32).astype(BF16)


def _inproj(x2d, gain, w_cat, w_t, tm):
    t = x2d.shape[0]
    tblk = lambda j: jnp.clip(j - (N_MAIN + 1), 0, N_TRANS - 1)
    return pl.pallas_call(
        _inproj_kernel,
        grid=(t // tm, N_MAIN + 1 + N_TRANS),
        in_specs=[
            pl.BlockSpec((tm, D_MODEL), lambda i, j: (i, 0)),
            pl.BlockSpec((1, D_MODEL), lambda i, j: (0, 0)),
            pl.BlockSpec((D_MODEL, 1024), lambda i, j: (0, jnp.minimum(j, N_MAIN))),
            pl.BlockSpec((1024, D_MODEL), lambda i, j: (tblk(j), 0)),
        ],
        out_specs=[
            pl.BlockSpec((tm, 1024), lambda i, j: (i, jnp.minimum(j, N_MAIN - 1))),
            pl.BlockSpec((tm, 1024), lambda i, j: (i, 0)),
            pl.BlockSpec((1024, tm), lambda i, j: (tblk(j), i)),
        ],
        out_shape=[
            jax.ShapeDtypeStruct((t, N_MAIN * 1024), BF16),
            jax.ShapeDtypeStruct((t, 1024), F32),
            jax.ShapeDtypeStruct((N_TRANS * 1024, t), BF16),
        ],
        scratch_shapes=[pltpu.VMEM((tm, D_MODEL), BF16)],
        compiler_params=_cparams(("parallel", "arbitrary")),
        name="inproj",
    )(x2d, gain, w_cat, w_t)


SSD_PAD = BF16_ROWS
HEADS_PER_GROUP = SSD_HEADS // SSD_GROUPS


def _ssd_constants():
    L = SSD_CHUNK
    shift = np.zeros((SSD_CONV * L, SSD_PAD + L), np.float32)
    for k in range(SSD_CONV):
        for t in range(L):
            shift[k * L + t, SSD_PAD + t - (SSD_CONV - 1) + k] = 1.0
    e_wide = np.zeros((3 * LANES, SSD_HEADS * LANES), np.float32)
    e_head = np.zeros((3 * LANES, SSD_HEADS * SSD_HEAD_DIM), np.float32)
    for part in range(3):
        for h in range(SSD_HEADS):
            e_wide[part * LANES + h, h * LANES:(h + 1) * LANES] = 1.0
            e_head[part * LANES + h, h * SSD_HEAD_DIM:(h + 1) * SSD_HEAD_DIM] = 1.0
    return (jnp.asarray(shift, BF16), jnp.asarray(e_wide, BF16), jnp.asarray(e_head, BF16))


def _split3(x):
    hi = x.astype(BF16)
    r1 = x - hi.astype(F32)
    mid = r1.astype(BF16)
    lo = (r1 - mid.astype(F32)).astype(BF16)
    return jnp.concatenate([hi, mid, lo], axis=-1)


SSD_STEP_CHUNKS = 2


def _ssd_kernel(za_ref, xs_ref, bc_ref, sm_ref, cw_ref, cb_ref, dtb_ref, alog_ref,
                dsk_ref, ng_ref, shift_ref, ewide_ref, ehead_ref, y_ref, xext, state):
    L = SSD_CHUNK
    rows = SSD_STEP_CHUNKS * L

    @pl.when(pl.program_id(1) == 0)
    def _():
        xext[0:SSD_PAD, :] = jnp.zeros((SSD_PAD, 2048), BF16)
        state[...] = jnp.zeros(state.shape, F32)

    xext[SSD_PAD:SSD_PAD + rows, 0:1024] = xs_ref[...]
    xext[SSD_PAD:SSD_PAD + rows, 1024:2048] = bc_ref[...]
    for ci in range(SSD_STEP_CHUNKS):
        _ssd_chunk(ci, za_ref, sm_ref, cw_ref, cb_ref, dtb_ref, alog_ref, dsk_ref, ng_ref,
                   shift_ref, ewide_ref, ehead_ref, y_ref, xext, state)
    xext[0:SSD_PAD, :] = xext[rows:rows + SSD_PAD, :]


def _ssd_chunk(ci, za_ref, sm_ref, cw_ref, cb_ref, dtb_ref, alog_ref, dsk_ref, ng_ref,
               shift_ref, ewide_ref, ehead_ref, y_ref, xext, state):
    L = SSD_CHUNK
    r0 = ci * L
    shifted = jnp.dot(shift_ref[...], xext[r0:r0 + SSD_PAD + L, :], preferred_element_type=F32)
    acc = cb_ref[...]
    for k in range(SSD_CONV):
        acc = acc + cw_ref[k:k + 1, :] * shifted[k * L:(k + 1) * L, :]
    xbc = _silu(acc)

    lane = lax.broadcasted_iota(jnp.int32, (1, LANES), 1)
    dt = jax.nn.softplus(sm_ref[r0:r0 + L, :] + dtb_ref[...])
    a = jnp.where(lane < SSD_HEADS, -jnp.exp(alog_ref[...]), 0.0)
    adt = dt * a
    row_i = lax.broadcasted_iota(jnp.int32, (L, L), 0)
    col_i = lax.broadcasted_iota(jnp.int32, (L, L), 1)
    causal = row_i >= col_i
    cs = jnp.dot(causal.astype(F32), adt, preferred_element_type=F32, precision=HI)
    cs_t = cs.T
    dt_t = dt.T
    cs3 = _split3(cs)
    cs_wide = jnp.dot(cs3, ewide_ref[...], preferred_element_type=F32)
    cs_head = jnp.dot(cs3, ehead_ref[...], preferred_element_type=F32)

    xs_all = xbc[:, 0:SSD_D_INNER]
    xs16 = xs_all.astype(BF16)
    pair_lane = lax.broadcasted_iota(jnp.int32, (1, 2 * SSD_HEAD_DIM), 1) < SSD_HEAD_DIM
    y_groups = []
    for g in range(SSD_GROUPS):
        b_g = xbc[:, SSD_D_INNER + g * SSD_STATE:SSD_D_INNER + (g + 1) * SSD_STATE]
        c_g = xbc[:, SSD_D_INNER + (SSD_GROUPS + g) * SSD_STATE:
                  SSD_D_INNER + (SSD_GROUPS + g + 1) * SSD_STATE]
        c_g16 = c_g.astype(BF16)
        b_gt = b_g.T
        cb = jnp.dot(c_g16, b_gt.astype(BF16), preferred_element_type=F32)
        st = state[g]
        y_off = jnp.dot(c_g16, st.astype(BF16), preferred_element_type=F32)
        y_pairs, st_pairs = [], []
        for pr in range(HEADS_PER_GROUP // 2):
            lhs = []
            for r in range(2):
                h = g * HEADS_PER_GROUP + 2 * pr + r
                row = cs_t[h:h + 1, :]
                last = cs_t[h:h + 1, L - 1:L]
                dt_row = dt_t[h:h + 1, :]
                decay = jnp.exp(jnp.where(causal, cs_wide[:, h * LANES:(h + 1) * LANES] - row, MASK_VALUE))
                lhs.append((cb * decay * dt_row).astype(BF16))
                lhs.append((b_gt * (jnp.exp(last - row) * dt_row)).astype(BF16))
            h0 = g * HEADS_PER_GROUP + 2 * pr
            res = jnp.dot(jnp.concatenate(lhs, axis=0), xs16[:, h0 * SSD_HEAD_DIM:(h0 + 2) * SSD_HEAD_DIM],
                          preferred_element_type=F32)
            y_pairs.append(jnp.where(pair_lane, res[0:L], res[2 * L:3 * L]))
            st_pairs.append(jnp.where(pair_lane, res[L:2 * L], res[3 * L:4 * L]))
        hsl = slice(g * HEADS_PER_GROUP * SSD_HEAD_DIM, (g + 1) * HEADS_PER_GROUP * SSD_HEAD_DIM)
        cs_g = cs_head[:, hsl]
        y_groups.append(jnp.concatenate(y_pairs, axis=-1) + jnp.exp(cs_g) * y_off)
        state[g] = st * jnp.exp(cs_g[L - 1:L, :]) + jnp.concatenate(st_pairs, axis=-1)

    y = jnp.concatenate(y_groups, axis=-1) + dsk_ref[...] * xs_all
    y = y * _silu(za_ref[r0:r0 + L, :].astype(F32))
    y_ref[r0:r0 + L, :] = _rms(y, ng_ref[...]).astype(BF16)


def _ssd(main, tail, conv_w, conv_b, dtb, alog, dsk, ng, bsz, seq):
    L = SSD_STEP_CHUNKS * SSD_CHUNK
    assert seq % L == 0
    nc = seq // L
    shift, e_wide, e_head = _ssd_constants()
    row = lambda b, c: b * nc + c
    vec = lambda n: pl.BlockSpec((1, n), lambda b, c: (0, 0))
    full = lambda a: pl.BlockSpec(a.shape, lambda b, c: (0, 0))
    return pl.pallas_call(
        _ssd_kernel,
        grid=(bsz, nc),
        in_specs=[
            pl.BlockSpec((L, 1024), lambda b, c: (row(b, c), 0)),
            pl.BlockSpec((L, 1024), lambda b, c: (row(b, c), 1)),
            pl.BlockSpec((L, 1024), lambda b, c: (row(b, c), 2)),
            pl.BlockSpec((L, LANES), lambda b, c: (row(b, c), TAIL_SMALL // LANES)),
            pl.BlockSpec((SSD_CONV, 2048), lambda b, c: (0, 0)),
            vec(2048), vec(LANES), vec(LANES), vec(1024), vec(1024),
            full(shift), full(e_wide), full(e_head),
        ],
        out_specs=pl.BlockSpec((L, 1024), lambda b, c: (row(b, c), 0)),
        out_shape=jax.ShapeDtypeStruct((bsz * seq, SSD_D_INNER), BF16),
        scratch_shapes=[
            pltpu.VMEM((SSD_PAD + L, 2048), BF16),
            pltpu.VMEM((SSD_GROUPS, SSD_STATE, HEADS_PER_GROUP * SSD_HEAD_DIM), F32),
        ],
        compiler_params=_cparams(("parallel", "arbitrary")),
        name="ssd",
    )(main, main, main, tail, conv_w, conv_b, dtb, alog, dsk, ng, shift, e_wide, e_head)


def _s5_pow(step, lr, li, k):
    mag = jnp.exp(lr * step * k)
    ang = li * step * k
    return mag * jnp.cos(ang), mag * jnp.sin(ang)


def _s5_kernel(ls_ref, lrow_ref, lcol_ref, bt_ref, cn_ref, ct_ref, u_ref, y_ref,
               k_scr, toep, bin_scr, cout, s_scr, xin_scr, *, bsz):
    P, H, L = S5_STATE, S5_GROUP, S5_L
    rows = u_ref.shape[1]
    nchunk = rows // bsz
    lane = lax.broadcasted_iota(jnp.int32, (1, 2 * P), 1)
    lo_lane = lane < P
    sub = lax.broadcasted_iota(jnp.int32, (2 * P, 1), 0)
    lo_sub = sub < P

    step = jnp.exp(ls_ref[0])
    lr, li = lrow_ref[0, 0:1, :], lrow_ref[0, 1:2, :]
    lr_c, li_c = lcol_ref[0, :, 0:1], lcol_ref[0, :, 1:2]

    abar_re, abar_im = _s5_pow(step, lr, li, 1.0)
    den = lr * lr + li * li
    nr, ni = abar_re - 1.0, abar_im
    f_re = (nr * lr + ni * li) / den
    f_im = (ni * lr - nr * li) / den
    br, bi = bt_ref[0, 0], bt_ref[0, 1]
    bb_re = f_re * br - f_im * bi
    bb_im = f_re * bi + f_im * br
    v1 = jnp.where(lo_lane, bb_re, bb_im)
    v2 = jnp.where(lo_lane, -bb_im, bb_re)

    cr, ci = cn_ref[0, 0], cn_ref[0, 1]
    g_rows = []
    for hp in range(H):
        g_re = bb_re[hp:hp + 1, :] * cr - bb_im[hp:hp + 1, :] * ci
        g_im = bb_re[hp:hp + 1, :] * ci + bb_im[hp:hp + 1, :] * cr
        g_rows.append(jnp.where(lo_lane, g_re, -g_im))
    g_mat = jnp.concatenate(g_rows, axis=0)
    lag = lax.broadcasted_iota(jnp.int32, (1, L), 1).astype(F32)
    p_re, p_im = _s5_pow(step, lr_c, li_c, lag)
    k_scr[...] = jnp.dot(g_mat, jnp.where(lo_sub, p_re, p_im), precision=HI,
                         preferred_element_type=F32)

    srow = lax.broadcasted_iota(jnp.int32, (L, L), 0)
    tcol = lax.broadcasted_iota(jnp.int32, (L, L), 1)
    keep = tcol >= srow

    def toep_rows(hp, carry):
        roff = pl.multiple_of(hp * L, L)
        koff = pl.multiple_of(hp * H, H)
        for h in range(H):
            kv = jnp.broadcast_to(k_scr[pl.ds(koff + h, 1), :], (L, L))
            blk = pltpu.roll(kv, 0, 1, stride=1, stride_axis=0)
            toep[pl.ds(roff, L), h * L:(h + 1) * L] = jnp.where(keep, blk, 0.0).astype(BF16)
        return carry

    lax.fori_loop(0, H, toep_rows, 0)

    s_col = lax.broadcasted_iota(jnp.int32, (L, 1), 0).astype(F32)
    q_re, q_im = _s5_pow(step, lr, li, (L - 1.0) - s_col)
    for hp in range(H):
        bin_scr[hp * L:(hp + 1) * L, :] = (q_re * v1[hp:hp + 1, :] + q_im * v2[hp:hp + 1, :]).astype(BF16)

    r_re, r_im = _s5_pow(step, lr_c, li_c, lag + 1.0)
    crt, cit = ct_ref[0, 0], ct_ref[0, 1]
    for h in range(H):
        c_r, c_i = crt[:, h:h + 1], cit[:, h:h + 1]
        cout[:, h * L:(h + 1) * L] = jnp.where(lo_sub, c_r * r_re - c_i * r_im,
                                               -(c_r * r_im + c_i * r_re)).astype(BF16)

    u = jnp.concatenate([u_ref[hp] for hp in range(H)], axis=-1)
    s_scr[...] = jnp.dot(u, bin_scr[...], preferred_element_type=F32)

    al_re, al_im = _s5_pow(step, lr, li, float(L))
    a1 = jnp.broadcast_to(al_re, (bsz, 2 * P))
    a2 = jnp.broadcast_to(jnp.where(lo_lane, -al_im, al_im), (bsz, 2 * P))

    x = xs = jnp.zeros((bsz, 2 * P), F32)
    for cidx in range(nchunk):
        sl = pl.ds(cidx, bsz, stride=nchunk)
        xin_scr[sl, :] = x
        s = s_scr[sl, :]
        x, xs = a1 * x + a2 * xs + s, a1 * xs - a2 * x + pltpu.roll(s, P, 1)

    y = jnp.dot(u, toep[...], preferred_element_type=F32)
    y = y + jnp.dot(xin_scr[...].astype(BF16), cout[...], preferred_element_type=F32)
    for h in range(H):
        y_ref[h] = y[:, h * L:(h + 1) * L].astype(BF16)


def _s5(ubt, ls, lrow, lcol, bt, cn, ct, bsz, seq):
    t = ubt.shape[1]
    rows = t // S5_L
    u3 = ubt.reshape(ubt.shape[0], rows, S5_L)
    P, H = S5_STATE, S5_GROUP
    g4 = lambda a, b: pl.BlockSpec((1, 2, a, b), lambda g: (g, 0, 0, 0))
    y3 = pl.pallas_call(
        functools.partial(_s5_kernel, bsz=bsz),
        grid=(S5_GROUPS,),
        in_specs=[pl.BlockSpec((1, 1, 1), lambda g: (g, 0, 0)),
                  pl.BlockSpec((1, 2, 2 * P), lambda g: (g, 0, 0)),
                  pl.BlockSpec((1, 2 * P, 2), lambda g: (g, 0, 0)),
                  g4(H, 2 * P), g4(H, 2 * P), g4(2 * P, H),
                  pl.BlockSpec((H, rows, S5_L), lambda g: (g, 0, 0))],
        out_specs=pl.BlockSpec((H, rows, S5_L), lambda g: (g, 0, 0)),
        out_shape=jax.ShapeDtypeStruct((S5_WIDTH, rows, S5_L), BF16),
        scratch_shapes=[pltpu.VMEM((H * H, S5_L), F32),
                        pltpu.VMEM((S5_K, S5_K), BF16),
                        pltpu.VMEM((S5_K, 2 * P), BF16),
                        pltpu.VMEM((2 * P, S5_K), BF16),
                        pltpu.VMEM((rows, 2 * P), F32),
                        pltpu.VMEM((rows, 2 * P), F32)],
        compiler_params=_cparams(("parallel",)),
        name="s5",
    )(ls, lrow, lcol, bt, cn, ct, u3)
    return y3.reshape(S5_WIDTH, t)


def _gelu_tanh(x):
    return 0.5 * x * (1.0 + jnp.tanh(0.7978845608028654 * (x + 0.044715 * (x * x * x))))


def _s5_post_kernel(y_ref, u_ref, z_ref, d_ref, wt_ref, b_ref, o_ref):
    y = y_ref[...].astype(F32) + d_ref[...] * u_ref[...].astype(F32)
    gl = _gelu_tanh(y)
    gate = jnp.dot(wt_ref[...], gl.astype(BF16), preferred_element_type=F32) + b_ref[...]
    out = gl * jax.nn.sigmoid(gate) * _silu(z_ref[...].astype(F32))
    o_ref[...] = out.T.astype(BF16)


def _s5_post(y_t, ubt, d_col, w_glu_t, b_col, tn):
    t = y_t.shape[1]
    colv = pl.BlockSpec((1024, 1), lambda i: (0, 0))
    return pl.pallas_call(
        _s5_post_kernel,
        grid=(t // tn,),
        in_specs=[pl.BlockSpec((1024, tn), lambda i: (0, i)),
                  pl.BlockSpec((1024, tn), lambda i: (0, i)),
                  pl.BlockSpec((1024, tn), lambda i: (1, i)),
                  colv,
                  pl.BlockSpec((1024, 1024), lambda i: (0, 0)),
                  colv],
        out_specs=pl.BlockSpec((tn, 1024), lambda i: (i, 0)),
        out_shape=jax.ShapeDtypeStruct((t, S5_WIDTH), BF16),
        compiler_params=_cparams(("parallel",)),
        name="s5_post",
    )(y_t, ubt, ubt, d_col, w_glu_t, b_col)


def _rope_tables_kernel(f_ref, fcol_ref, cos_ref, sa_ref, sb_ref, cost_ref, sint_ref):
    ts = cos_ref.shape[0]
    base = pl.program_id(0) * ts
    pos = (base + lax.broadcasted_iota(jnp.int32, (ts, 1), 0)).astype(F32)
    ang = pos * f_ref[...]
    cos, sin = jnp.cos(ang), jnp.sin(ang)
    lane = lax.broadcasted_iota(jnp.int32, (1, HEAD_PAD), 1)
    half = MLA_ROPE // 2
    lo = (lane >= ROPE_LO) & (lane < ROPE_LO + half)
    hi = (lane >= ROPE_LO + half) & (lane < ROPE_LO + MLA_ROPE)
    cos_ref[...] = jnp.where(lo | hi, cos, 0.0)
    sa_ref[...] = jnp.where(lo, -sin, 0.0)
    sb_ref[...] = jnp.where(hi, sin, 0.0)
    pos_t = (base + lax.broadcasted_iota(jnp.int32, (1, ts), 1)).astype(F32)
    ang_t = fcol_ref[...] * pos_t
    cost_ref[...] = jnp.cos(ang_t)
    sint_ref[...] = jnp.sin(ang_t)


def _rope_tables(freq_lane, freq_col, seq, ts):
    half = MLA_ROPE // 2
    out = jax.ShapeDtypeStruct((seq, HEAD_PAD), F32)
    out_t = jax.ShapeDtypeStruct((half, seq), F32)
    blk = pl.BlockSpec((ts, HEAD_PAD), lambda i: (i, 0))
    blk_t = pl.BlockSpec((half, ts), lambda i: (0, i))
    return pl.pallas_call(
        _rope_tables_kernel,
        grid=(seq // ts,),
        in_specs=[pl.BlockSpec((1, HEAD_PAD), lambda i: (0, 0)),
                  pl.BlockSpec((half, 1), lambda i: (0, 0))],
        out_specs=[blk, blk, blk, blk_t, blk_t],
        out_shape=[out, out, out, out_t, out_t],
        compiler_params=_cparams(("parallel",)),
        name="rope_tables",
    )(freq_lane, freq_col)


def _rope(x, cos, sa, sb):
    half = MLA_ROPE // 2
    return (x * cos + pltpu.roll(x, HEAD_PAD - half, 1) * sa + pltpu.roll(x, half, 1) * sb)


ATT_T = 512
ATT_HP = 2
V_ROWS = MLA_V + BF16_ROWS
LOG2E = 1.4426950408889634


def _mla_prep_kernel(cq_ref, ckv_ref, sm_ref, qn_ref, kvn_ref, wqt_ref, wk_ref, wvt_ref,
                     cos_ref, sa_ref, sb_ref, cost_ref, sint_ref, qt_ref, k_ref, vt_ref):
    tm = cq_ref.shape[0]
    half = MLA_ROPE // 2
    scale = (MLA_NOPE + MLA_ROPE) ** -0.5 * LOG2E
    qn = _rms(cq_ref[...], qn_ref[...]).astype(BF16)
    q_t = lax.dot_general(wqt_ref[...], qn, NT, preferred_element_type=F32)
    c_t, s_t = cost_ref[...], sint_ref[...]
    for h in range(MLA_HEADS):
        r0 = h * HEAD_PAD
        r1 = r0 + ROPE_LO
        x1, x2 = q_t[r1:r1 + half, :], q_t[r1 + half:r1 + 2 * half, :]
        pieces = [(r0, (q_t[r0:r1, :] * scale).astype(BF16)),
                  (r1, ((x1 * c_t - x2 * s_t) * scale).astype(BF16)),
                  (r1 + half, ((x1 * s_t + x2 * c_t) * scale).astype(BF16)),
                  (r1 + 2 * half, jnp.zeros((HEAD_PAD - ROPE_LO - 2 * half, tm), BF16))]
        for j in range(tm // ATT_T):
            for row, val in pieces:
                qt_ref[j, row:row + val.shape[0], :] = val[:, j * ATT_T:(j + 1) * ATT_T]

    kvn = _rms(ckv_ref[...], kvn_ref[...]).astype(BF16)
    k = jnp.dot(kvn, wk_ref[...], preferred_element_type=F32)
    kr = _rope(sm_ref[...], cos_ref[...], sa_ref[...], sb_ref[...])
    for h in range(MLA_HEADS):
        sl = slice(h * HEAD_PAD, (h + 1) * HEAD_PAD)
        k_ref[:, sl] = (k[:, sl] + kr).astype(BF16)
    v_t = lax.dot_general(wvt_ref[...], kvn, NT, preferred_element_type=F32)
    ones_row = (lax.broadcasted_iota(jnp.int32, (BF16_ROWS, ATT_T), 0) == 0).astype(BF16)
    for j in range(tm // ATT_T):
        for h in range(MLA_HEADS):
            vt_ref[j, h * V_ROWS:h * V_ROWS + MLA_V, :] = (
                v_t[h * MLA_V:(h + 1) * MLA_V, j * ATT_T:(j + 1) * ATT_T].astype(BF16))
            vt_ref[j, h * V_ROWS + MLA_V:(h + 1) * V_ROWS, :] = ones_row


def _mla_prep(tail, qn, kvn, wqt, wk, wvt, tables, seq, tm):
    cos, sa, sb, cost, sint = tables
    t = tail.shape[0]
    npos = seq // tm
    half = MLA_ROPE // 2
    tab = pl.BlockSpec((tm, HEAD_PAD), lambda i: (i % npos, 0))
    tab_t = pl.BlockSpec((half, tm), lambda i: (0, i % npos))
    full = lambda a, b: pl.BlockSpec((a, b), lambda i: (0, 0))
    hw = MLA_HEADS * HEAD_PAD
    nkv = tm // ATT_T
    return pl.pallas_call(
        _mla_prep_kernel,
        grid=(t // tm,),
        in_specs=[pl.BlockSpec((tm, MLA_Q_RANK), lambda i: (i, TAIL_CQ // MLA_Q_RANK)),
                  pl.BlockSpec((tm, MLA_KV_RANK), lambda i: (i, TAIL_CKV // MLA_KV_RANK)),
                  pl.BlockSpec((tm, LANES), lambda i: (i, TAIL_SMALL // LANES)),
                  full(1, MLA_Q_RANK), full(1, MLA_KV_RANK),
                  full(hw, MLA_Q_RANK), full(MLA_KV_RANK, hw), full(MLA_WIDTH, MLA_KV_RANK),
                  tab, tab, tab, tab_t, tab_t],
        out_specs=[pl.BlockSpec((nkv, hw, ATT_T), lambda i: (i, 0, 0)),
                   pl.BlockSpec((tm, hw), lambda i: (i, 0)),
                   pl.BlockSpec((nkv, MLA_HEADS * V_ROWS, ATT_T), lambda i: (i, 0, 0))],
        out_shape=[jax.ShapeDtypeStruct((t // ATT_T, hw, ATT_T), BF16),
                   jax.ShapeDtypeStruct((t, hw), BF16),
                   jax.ShapeDtypeStruct((t // ATT_T, MLA_HEADS * V_ROWS, ATT_T), BF16)],
        compiler_params=_cparams(("parallel",)),
        name="mla_prep",
    )(tail, tail, tail, qn, kvn, wqt, wk, wvt, cos, sa, sb, cost, sint)


ATT_QT = 4


def _attn_kernel(qt_ref, k_ref, vt_ref, o_ref, acc_scr, m_scr, s0, s1, mb0, mb1):
    t = ATT_T
    nq = qt_ref.shape[0]
    npos = 2 * nq + 2
    a = 2 * pl.program_id(2)
    tiles = (a, nq - 1 - a, a + 1, nq - 2 - a)
    m_scr[...] = jnp.full(m_scr.shape, MASK_VALUE, F32)
    acc_scr[...] = jnp.zeros(acc_scr.shape, F32)

    def locate(pos):
        if isinstance(pos, int) and pos < ATT_QT:
            return pos, tiles[pos], tiles[pos]
        r = pos - ATT_QT
        c0, c1, c2 = tiles[0], tiles[0] + tiles[1], tiles[0] + tiles[1] + tiles[2]
        slot = (r >= c0).astype(jnp.int32) + (r >= c1).astype(jnp.int32) + (r >= c2).astype(jnp.int32)
        start = jnp.where(slot == 0, 0, jnp.where(slot == 1, c0, jnp.where(slot == 2, c1, c2)))
        qtile = jnp.where(slot == 0, tiles[0], jnp.where(slot == 1, tiles[1],
                                                         jnp.where(slot == 2, tiles[2], tiles[3])))
        return slot, qtile, r - start

    def scores(pos, s_buf, mb_buf):
        _, qtile, kidx = locate(pos)
        masked = isinstance(pos, int) and pos < ATT_QT
        off = pl.multiple_of(kidx * t, t)
        for hh in range(ATT_HP):
            q_t = qt_ref[qtile, hh * HEAD_PAD:(hh + 1) * HEAD_PAD, :]
            k = k_ref[pl.ds(off, t), hh * HEAD_PAD:(hh + 1) * HEAD_PAD]
            s = jnp.dot(k, q_t, preferred_element_type=F32)
            if masked:
                kv_i = lax.broadcasted_iota(jnp.int32, (t, t), 0)
                q_i = lax.broadcasted_iota(jnp.int32, (t, t), 1)
                s = jnp.where(kv_i <= q_i, s, MASK_VALUE)
            s_buf[hh] = s
            mb_buf[hh:hh + 1, :] = jnp.max(s, axis=0, keepdims=True)

    def update(pos, s_buf, mb_buf):
        slot, _, kidx = locate(pos)
        for hh in range(ATT_HP):
            rows = slice(hh * V_ROWS, (hh + 1) * V_ROWS)
            m_old = m_scr[slot, hh:hh + 1, :]
            m_new = jnp.maximum(m_old, mb_buf[hh:hh + 1, :])
            alpha = jnp.exp2(m_old - m_new)
            p = jnp.exp2(s_buf[hh] - m_new).astype(BF16)
            m_scr[slot, hh:hh + 1, :] = m_new
            pv = jnp.dot(vt_ref[kidx, rows, :], p, preferred_element_type=F32)
            acc_scr[slot, rows, :] = alpha * acc_scr[slot, rows, :] + pv

    def pair(i):
        scores(2 * i + 1, s1, mb1)
        update(2 * i, s0, mb0)
        scores(2 * i + 2, s0, mb0)
        update(2 * i + 1, s1, mb1)

    scores(0, s0, mb0)
    for i in range(ATT_QT // 2):
        pair(i)

    first = ATT_QT // 2
    n_pairs = npos // 2 - 1 - first
    unroll = max(u for u in (7, 2, 1) if n_pairs % u == 0)

    def body(j, carry):
        for u in range(unroll):
            pair(first + unroll * j + u)
        return carry

    lax.fori_loop(0, n_pairs // unroll, body, 0)
    scores(npos - 1, s1, mb1)
    update(npos - 2, s0, mb0)
    update(npos - 1, s1, mb1)

    for slot in range(ATT_QT):
        outs = []
        for hh in range(ATT_HP):
            r0 = hh * V_ROWS
            outs.append(acc_scr[slot, r0:r0 + MLA_V, :] / acc_scr[slot, r0 + MLA_V:r0 + MLA_V + 1, :])
        off = pl.multiple_of(tiles[slot] * t, t)
        o_ref[pl.ds(off, t), :] = jnp.concatenate(outs, axis=0).T.astype(BF16)


def _mla_attn(q_t, k, v_t, bsz, seq):
    nq = seq // ATT_T
    nhp = MLA_HEADS // ATT_HP
    assert nq % ATT_QT == 0
    return pl.pallas_call(
        _attn_kernel,
        grid=(bsz, nhp, nq // ATT_QT),
        in_specs=[pl.BlockSpec((nq, ATT_HP * HEAD_PAD, ATT_T), lambda b, h, i: (b, h, 0)),
                  pl.BlockSpec((seq, ATT_HP * HEAD_PAD), lambda b, h, i: (b, h)),
                  pl.BlockSpec((nq, ATT_HP * V_ROWS, ATT_T), lambda b, h, i: (b, h, 0))],
        out_specs=pl.BlockSpec((seq, ATT_HP * MLA_V), lambda b, h, i: (b, h)),
        out_shape=jax.ShapeDtypeStruct((bsz * seq, MLA_WIDTH), BF16),
        scratch_shapes=[pltpu.VMEM((ATT_QT, ATT_HP * V_ROWS, ATT_T), F32),
                        pltpu.VMEM((ATT_QT, 8, ATT_T), F32),
                        pltpu.VMEM((ATT_HP, ATT_T, ATT_T), F32),
                        pltpu.VMEM((ATT_HP, ATT_T, ATT_T), F32),
                        pltpu.VMEM((8, ATT_T), F32),
                        pltpu.VMEM((8, ATT_T), F32)],
        compiler_params=_cparams(("parallel", "parallel", "arbitrary")),
        name="mla_attn",
    )(q_t, k, v_t)


def _merge_kernel(x_ref, ya_ref, yb_ref, o_ref, zc_ref, g0_ref, g1_ref, g2_ref,
                  wa_ref, wb_ref, wc_ref, wo_ref, pn_ref, out_ref):
    dot = lambda a, w: jnp.dot(a, w[...], preferred_element_type=F32)
    sig = lambda r: jax.nn.sigmoid(r[...].astype(F32))
    yc = (o_ref[...].astype(F32) * _silu(zc_ref[...].astype(F32))).astype(BF16)
    merged = (sig(g0_ref) * dot(ya_ref[...], wa_ref)
              + sig(g1_ref) * dot(yb_ref[...], wb_ref)
              + sig(g2_ref) * dot(yc, wc_ref))
    out = dot(merged.astype(BF16), wo_ref)
    out_ref[...] = x_ref[...] + _rms(out, pn_ref[...])


def _merge(x2d, ya, yb, o, main, wa, wb, wc, wo, pn, tm):
    t = x2d.shape[0]
    col = lambda j: pl.BlockSpec((tm, 1024), lambda i: (i, j))
    wfull = pl.BlockSpec((1024, 1024), lambda i: (0, 0))
    return pl.pallas_call(
        _merge_kernel,
        grid=(t // tm,),
        in_specs=[col(0), col(0), col(0), col(0), col(MAIN_ZC), col(MAIN_G0), col(MAIN_G0 + 1),
                  col(MAIN_G0 + 2), wfull, wfull, wfull, wfull,
                  pl.BlockSpec((1, 1024), lambda i: (0, 0))],
        out_specs=col(0),
        out_shape=jax.ShapeDtypeStruct((t, D_MODEL), F32),
        compiler_params=_cparams(("parallel",)),
        name="merge",
    )(x2d, ya, yb, o, main, main, main, main, wa, wb, wc, wo, pn)


def _pack_w_in(w):
    sizes = (SSD_D_INNER, SSD_D_INNER + 2 * SSD_GROUPS * SSD_STATE, SSD_HEADS, S5_WIDTH, S5_WIDTH,
             MLA_Q_RANK, MLA_KV_RANK, MLA_ROPE, MLA_WIDTH, 3 * D_MODEL)
    parts, start = [], 0
    for n in sizes:
        parts.append(w[:, start:start + n])
        start += n
    z_a, xbc, dt, u_b, z_b, c_q, c_kv, k_rope, z_c, gates = parts
    zeros = lambda n: jnp.zeros((w.shape[0], n), w.dtype)
    small = jnp.concatenate([dt, zeros(ROPE_LO - SSD_HEADS), k_rope,
                             zeros(LANES - ROPE_LO - MLA_ROPE)], axis=1)
    tail = jnp.concatenate([c_q, c_kv, small, zeros(1024 - TAIL_SMALL - LANES)], axis=1)
    w_cat = jnp.concatenate([z_a, xbc, z_c, gates, tail], axis=1).astype(BF16)
    w_t = jnp.concatenate([u_b, z_b], axis=1).T.astype(BF16)
    return w_cat, w_t


def _pad_heads(w, lo, width):
    r = w.shape[0]
    w3 = w.reshape(r, MLA_HEADS, -1)[:, :, lo:lo + width]
    w3 = jnp.pad(w3, ((0, 0), (0, 0), (0, HEAD_PAD - width)))
    return w3.reshape(r, MLA_HEADS * HEAD_PAD)


def _row(v, n=None):
    v = v.reshape(1, -1).astype(F32)
    if n is not None and v.shape[1] < n:
        v = jnp.pad(v, ((0, 0), (0, n - v.shape[1])))
    return v


def _dup(a, axis):
    return jnp.concatenate([a, a], axis=axis).astype(F32)


def kernel(x, pre_norm, w_in, conv_w, conv_b, dt_bias, a_log, d_ssd, ssd_norm, w_a, s5_log_step, s5_lambda_re, s5_lambda_im, s5_b_re, s5_b_im, s5_c_re, s5_c_im, s5_d, w_glu, b_glu, w_b, q_norm, w_uq, kv_norm, w_ukv, w_c, w_o, post_norm):
    bsz, seq, _ = x.shape
    depth = w_in.shape[0]
    t = bsz * seq
    assert seq % ATT_T == 0 and seq % SSD_CHUNK == 0 and seq % S5_L == 0
    tm = min(1024, seq)

    half = MLA_ROPE // 2
    inv_freq = ROPE_THETA ** (-jnp.arange(0, MLA_ROPE, 2, dtype=F32) / MLA_ROPE)
    freq_lane = jnp.concatenate([jnp.zeros((ROPE_LO,), F32), inv_freq, inv_freq,
                                 jnp.zeros((HEAD_PAD - ROPE_LO - 2 * half,), F32)]).reshape(1, HEAD_PAD)
    tables = _rope_tables(freq_lane, inv_freq.reshape(half, 1), seq, min(512, seq))

    x2d = x.reshape(t, D_MODEL)
    for l in range(depth):
        w_cat, w_t = _pack_w_in(w_in[l])
        main, tail, ubt = _inproj(x2d, _row(pre_norm[l]), w_cat, w_t, tm)

        dsk = jnp.repeat(d_ssd[l].astype(F32), SSD_HEAD_DIM).reshape(1, SSD_D_INNER)
        y_a = _ssd(main, tail, conv_w[l].astype(F32), _row(conv_b[l]), _row(dt_bias[l], LANES),
                   _row(a_log[l], LANES), dsk, _row(ssd_norm[l]), bsz, seq)

        ls3 = s5_log_step[l].astype(F32).reshape(S5_GROUPS, 1, 1)
        lam = jnp.stack([s5_lambda_re[l], s5_lambda_im[l]], axis=1)
        lrow = _dup(lam, 2)
        lcol = jnp.swapaxes(lrow, 1, 2)
        bt = _dup(jnp.stack([jnp.swapaxes(s5_b_re[l], 1, 2),
                             jnp.swapaxes(s5_b_im[l], 1, 2)], axis=1), 3)
        cn = _dup(jnp.stack([s5_c_re[l], s5_c_im[l]], axis=1), 3)
        ct = jnp.swapaxes(cn, 2, 3)
        y_t = _s5(ubt, ls3, lrow, lcol, bt, cn, ct, bsz, seq)
        y_b = _s5_post(y_t, ubt, s5_d[l].astype(F32).reshape(S5_WIDTH, 1), w_glu[l].T.astype(BF16),
                       b_glu[l].astype(F32).reshape(S5_WIDTH, 1), min(512, seq))

        stride_q = MLA_NOPE + MLA_ROPE
        wqt = _pad_heads(w_uq[l], 0, stride_q).T.astype(BF16)
        wk = _pad_heads(w_ukv[l], 0, MLA_NOPE).astype(BF16)
        wvt = (w_ukv[l].reshape(MLA_KV_RANK, MLA_HEADS, MLA_NOPE + MLA_V)[:, :, MLA_NOPE:]
               .reshape(MLA_KV_RANK, MLA_WIDTH).T.astype(BF16))
        q_t, k, v_t = _mla_prep(tail, _row(q_norm[l]), _row(kv_norm[l]), wqt, wk, wvt,
                                tables, seq, tm)
        o = _mla_attn(q_t, k, v_t, bsz, seq)

        x2d = _merge(x2d, y_a, y_b, o, main, w_a[l].astype(BF16), w_b[l].astype(BF16),
                     w_c[l].astype(BF16), w_o[l].astype(BF16), _row(post_norm[l]), min(512, seq))
    return x2d.reshape(bsz, seq, D_MODEL)
```

```python
import functools

import numpy as np

import jax
import jax.numpy as jnp
from jax import lax
from jax.experimental import pallas as pl
from jax.experimental.pallas import tpu as pltpu

F32 = jnp.float32
BF16 = jnp.bfloat16

D_MODEL = 1024
SSD_D_INNER = 1024
SSD_HEAD_DIM = 64
SSD_HEADS = 16
SSD_GROUPS = 4
SSD_STATE = 128
SSD_CONV = 4
SSD_CHUNK = 128
S5_WIDTH = 1024
S5_GROUP = 16
S5_GROUPS = 64
S5_STATE = 64
MLA_HEADS = 16
MLA_NOPE = 64
MLA_ROPE = 32
MLA_V = 64
MLA_Q_RANK = 512
MLA_KV_RANK = 256
MLA_WIDTH = 1024
ROPE_THETA = 10000.0
EPS = 1e-6
MASK_VALUE = -1e30

LANES = 128
BF16_ROWS = 16
S5_L = LANES
S5_K = S5_L * S5_GROUP
HEAD_PAD = 128
ROPE_LO = MLA_NOPE
VMEM_LIMIT = 56 * 1024 * 1024
HI = lax.Precision.HIGHEST
NT = (((1,), (1,)), ((), ()))


def _cparams(sem):
    return pltpu.CompilerParams(dimension_semantics=sem, vmem_limit_bytes=VMEM_LIMIT)


def _silu(x):
    h = 0.5 * x
    return h * jnp.tanh(h) + h


def _rms(x, g):
    return x * lax.rsqrt(jnp.mean(x * x, axis=-1, keepdims=True) + EPS) * g


N_MAIN = 7
N_TRANS = 2
MAIN_ZC, MAIN_G0 = 3, 4
TAIL_CQ, TAIL_CKV, TAIL_SMALL = 0, 512, 768


def _inproj_kernel(x_ref, g_ref, w_ref, wt_ref, main_ref, tail_ref, ubt_ref, h_scr):
    j = pl.program_id(1)

    @pl.when(j == 0)
    def _():
        h_scr[...] = _rms(x_ref[...], g_ref[...]).astype(BF16)

    @pl.when(j < N_MAIN)
    def _():
        main_ref[...] = jnp.dot(h_scr[...], w_ref[...], preferred_element_type=F32).astype(BF16)

    @pl.when(j == N_MAIN)
    def _():
        tail_ref[...] = jnp.dot(h_scr[...], w_ref[...], preferred_element_type=F32)

    @pl.when(j > N_MAIN)
    def _():
        ubt_ref[...] = lax.dot_general(wt_ref[...], h_scr[...], NT,
                                       preferred_element_type=F32).astype(BF16)


def _inproj(x2d, gain, w_cat, w_t, layer, tm):
    t = x2d.shape[0]
    tblk = lambda j: jnp.clip(j - (N_MAIN + 1), 0, N_TRANS - 1)
    return pl.pallas_call(
        _inproj_kernel,
        grid=(t // tm, N_MAIN + 1 + N_TRANS),
        in_specs=[
            pl.BlockSpec((tm, D_MODEL), lambda i, j: (i, 0)),
            pl.BlockSpec((1, D_MODEL), lambda i, j: (0, 0)),
            pl.BlockSpec((None, D_MODEL, 1024), lambda i, j: (layer, 0, jnp.minimum(j, N_MAIN))),
            pl.BlockSpec((None, 1024, D_MODEL), lambda i, j: (layer, tblk(j), 0)),
        ],
        out_specs=[
            pl.BlockSpec((tm, 1024), lambda i, j: (i, jnp.minimum(j, N_MAIN - 1))),
            pl.BlockSpec((tm, 1024), lambda i, j: (i, 0)),
            pl.BlockSpec((1024, tm), lambda i, j: (tblk(j), i)),
        ],
        out_shape=[
            jax.ShapeDtypeStruct((t, N_MAIN * 1024), BF16),
            jax.ShapeDtypeStruct((t, 1024), F32),
            jax.ShapeDtypeStruct((N_TRANS * 1024, t), BF16),
        ],
        scratch_shapes=[pltpu.VMEM((tm, D_MODEL), BF16)],
        compiler_params=_cparams(("parallel", "arbitrary")),
        name="inproj",
    )(x2d, gain, w_cat, w_t)


SSD_PAD = BF16_ROWS
HEADS_PER_GROUP = SSD_HEADS // SSD_GROUPS


def _ssd_constants():
    L = SSD_CHUNK
    win = SSD_PAD + L
    shift = np.zeros((L, SSD_CONV * win), np.float32)
    for k in range(SSD_CONV):
        for t in range(L):
            shift[t, k * win + SSD_PAD + t - (SSD_CONV - 1) + k] = 1.0
    e_wide = np.zeros((3 * LANES, SSD_HEADS * LANES), np.float32)
    e_head = np.zeros((3 * LANES, SSD_HEADS * SSD_HEAD_DIM), np.float32)
    for part in range(3):
        for h in range(SSD_HEADS):
            e_wide[part * LANES + h, h * LANES:(h + 1) * LANES] = 1.0
            e_head[part * LANES + h, h * SSD_HEAD_DIM:(h + 1) * SSD_HEAD_DIM] = 1.0
    return (jnp.asarray(shift, BF16), jnp.asarray(e_wide, BF16), jnp.asarray(e_head, BF16))


def _split3(x):
    hi = x.astype(BF16)
    r1 = x - hi.astype(F32)
    mid = r1.astype(BF16)
    lo = (r1 - mid.astype(F32)).astype(BF16)
    return jnp.concatenate([hi, mid, lo], axis=-1)


SSD_STEP_CHUNKS = 2


def _ssd_kernel(za_ref, xs_ref, bc_ref, sm_ref, cw_ref, cb_ref, dtb_ref, alog_ref,
                dsk_ref, ng_ref, shift_ref, ewide_ref, ehead_ref, y_ref, xext, state):
    L = SSD_CHUNK
    rows = SSD_STEP_CHUNKS * L

    @pl.when(pl.program_id(1) == 0)
    def _():
        xext[0:SSD_PAD, :] = jnp.zeros((SSD_PAD, 2048), BF16)
        state[...] = jnp.zeros(state.shape, F32)

    xext[SSD_PAD:SSD_PAD + rows, 0:1024] = xs_ref[...]
    xext[SSD_PAD:SSD_PAD + rows, 1024:2048] = bc_ref[...]
    for ci in range(SSD_STEP_CHUNKS):
        _ssd_chunk(ci, za_ref, sm_ref, cw_ref, cb_ref, dtb_ref, alog_ref, dsk_ref, ng_ref,
                   shift_ref, ewide_ref, ehead_ref, y_ref, xext, state)
    xext[0:SSD_PAD, :] = xext[rows:rows + SSD_PAD, :]


def _ssd_chunk(ci, za_ref, sm_ref, cw_ref, cb_ref, dtb_ref, alog_ref, dsk_ref, ng_ref,
               shift_ref, ewide_ref, ehead_ref, y_ref, xext, state):
    L = SSD_CHUNK
    r0 = ci * L
    window = xext[r0:r0 + SSD_PAD + L, :]
    taps = jnp.concatenate([window * cw_ref[k:k + 1, :] for k in range(SSD_CONV)], axis=0)
    acc = jnp.dot(shift_ref[...], taps, preferred_element_type=F32) + cb_ref[...]
    xbc = _silu(acc)

    lane = lax.broadcasted_iota(jnp.int32, (1, LANES), 1)
    dt = jax.nn.softplus(sm_ref[r0:r0 + L, :] + dtb_ref[...])
    a = jnp.where(lane < SSD_HEADS, -jnp.exp(alog_ref[...]), 0.0)
    adt = dt * a
    row_i = lax.broadcasted_iota(jnp.int32, (L, L), 0)
    col_i = lax.broadcasted_iota(jnp.int32, (L, L), 1)
    causal = row_i >= col_i
    tri = causal.astype(BF16)
    adt3 = _split3(adt)
    cs = jnp.dot(jnp.concatenate([tri, tri, tri], axis=1),
                 jnp.concatenate([adt3[:, 0:LANES], adt3[:, LANES:2 * LANES], adt3[:, 2 * LANES:]], axis=0),
                 preferred_element_type=F32)
    cs_t = cs.T
    dt_t = dt.T
    cs3 = _split3(cs)
    cs_wide = jnp.dot(cs3, ewide_ref[...], preferred_element_type=F32)
    cs_head = jnp.dot(cs3, ehead_ref[...], preferred_element_type=F32)

    xs_all = xbc[:, 0:SSD_D_INNER]
    xs16 = xs_all.astype(BF16)
    pair_lane = lax.broadcasted_iota(jnp.int32, (1, 2 * SSD_HEAD_DIM), 1) < SSD_HEAD_DIM
    y_groups = []
    for g in range(SSD_GROUPS):
        b_g = xbc[:, SSD_D_INNER + g * SSD_STATE:SSD_D_INNER + (g + 1) * SSD_STATE]
        c_g = xbc[:, SSD_D_INNER + (SSD_GROUPS + g) * SSD_STATE:
                  SSD_D_INNER + (SSD_GROUPS + g + 1) * SSD_STATE]
        c_g16 = c_g.astype(BF16)
        b_gt = b_g.T
        cb = jnp.dot(c_g16, b_gt.astype(BF16), preferred_element_type=F32)
        st = state[g]
        y_off = jnp.dot(c_g16, st.astype(BF16), preferred_element_type=F32)
        y_pairs, st_pairs = [], []
        for pr in range(HEADS_PER_GROUP // 2):
            lhs = []
            for r in range(2):
                h = g * HEADS_PER_GROUP + 2 * pr + r
                row = cs_t[h:h + 1, :]
                last = cs_t[h:h + 1, L - 1:L]
                dt_row = dt_t[h:h + 1, :]
                decay = jnp.exp(jnp.where(causal, cs_wide[:, h * LANES:(h + 1) * LANES] - row, MASK_VALUE))
                lhs.append((cb * decay * dt_row).astype(BF16))
                lhs.append((b_gt * (jnp.exp(last - row) * dt_row)).astype(BF16))
            h0 = g * HEADS_PER_GROUP + 2 * pr
            res = jnp.dot(jnp.concatenate(lhs, axis=0), xs16[:, h0 * SSD_HEAD_DIM:(h0 + 2) * SSD_HEAD_DIM],
                          preferred_element_type=F32)
            y_pairs.append(jnp.where(pair_lane, res[0:L], res[2 * L:3 * L]))
            st_pairs.append(jnp.where(pair_lane, res[L:2 * L], res[3 * L:4 * L]))
        hsl = slice(g * HEADS_PER_GROUP * SSD_HEAD_DIM, (g + 1) * HEADS_PER_GROUP * SSD_HEAD_DIM)
        cs_g = cs_head[:, hsl]
        y_groups.append(jnp.concatenate(y_pairs, axis=-1) + jnp.exp(cs_g) * y_off)
        state[g] = st * jnp.exp(cs_g[L - 1:L, :]) + jnp.concatenate(st_pairs, axis=-1)

    y = jnp.concatenate(y_groups, axis=-1) + dsk_ref[...] * xs_all
    y = y * _silu(za_ref[r0:r0 + L, :].astype(F32))
    y_ref[r0:r0 + L, :] = _rms(y, ng_ref[...]).astype(BF16)


def _ssd(main, tail, conv_w, conv_b, dtb, alog, dsk, ng, bsz, seq):
    L = SSD_STEP_CHUNKS * SSD_CHUNK
    assert seq % L == 0
    nc = seq // L
    shift, e_wide, e_head = _ssd_constants()
    row = lambda b, c: b * nc + c
    vec = lambda n: pl.BlockSpec((1, n), lambda b, c: (0, 0))
    full = lambda a: pl.BlockSpec(a.shape, lambda b, c: (0, 0))
    return pl.pallas_call(
        _ssd_kernel,
        grid=(bsz, nc),
        in_specs=[
            pl.BlockSpec((L, 1024), lambda b, c: (row(b, c), 0)),
            pl.BlockSpec((L, 1024), lambda b, c: (row(b, c), 1)),
            pl.BlockSpec((L, 1024), lambda b, c: (row(b, c), 2)),
            pl.BlockSpec((L, LANES), lambda b, c: (row(b, c), TAIL_SMALL // LANES)),
            pl.BlockSpec((SSD_CONV, 2048), lambda b, c: (0, 0)),
            vec(2048), vec(LANES), vec(LANES), vec(1024), vec(1024),
            full(shift), full(e_wide), full(e_head),
        ],
        out_specs=pl.BlockSpec((L, 1024), lambda b, c: (row(b, c), 0)),
        out_shape=jax.ShapeDtypeStruct((bsz * seq, SSD_D_INNER), BF16),
        scratch_shapes=[
            pltpu.VMEM((SSD_PAD + L, 2048), BF16),
            pltpu.VMEM((SSD_GROUPS, SSD_STATE, HEADS_PER_GROUP * SSD_HEAD_DIM), F32),
        ],
        compiler_params=_cparams(("parallel", "arbitrary")),
        name="ssd",
    )(main, main, main, tail, conv_w, conv_b, dtb, alog, dsk, ng, shift, e_wide, e_head)


def _s5_pow(step, lr, li, k):
    mag = jnp.exp(lr * step * k)
    ang = li * step * k
    return mag * jnp.cos(ang), mag * jnp.sin(ang)


def _s5_kernel(ls_ref, lrow_ref, lcol_ref, bt_ref, cn_ref, ct_ref, u_ref, y_ref,
               k_scr, toep, bin_scr, cout, s_scr, xin_scr, *, bsz):
    P, H, L = S5_STATE, S5_GROUP, S5_L
    rows = u_ref.shape[1]
    nchunk = rows // bsz
    lane = lax.broadcasted_iota(jnp.int32, (1, 2 * P), 1)
    lo_lane = lane < P
    sub = lax.broadcasted_iota(jnp.int32, (2 * P, 1), 0)
    lo_sub = sub < P

    step = jnp.exp(ls_ref[0])
    lr, li = lrow_ref[0, 0:1, :], lrow_ref[0, 1:2, :]
    lr_c, li_c = lcol_ref[0, :, 0:1], lcol_ref[0, :, 1:2]

    abar_re, abar_im = _s5_pow(step, lr, li, 1.0)
    den = lr * lr + li * li
    nr, ni = abar_re - 1.0, abar_im
    f_re = (nr * lr + ni * li) / den
    f_im = (ni * lr - nr * li) / den
    br, bi = bt_ref[0, 0], bt_ref[0, 1]
    bb_re = f_re * br - f_im * bi
    bb_im = f_re * bi + f_im * br
    v1 = jnp.where(lo_lane, bb_re, bb_im)
    v2 = jnp.where(lo_lane, -bb_im, bb_re)

    cr, ci = cn_ref[0, 0], cn_ref[0, 1]
    g_rows = []
    for hp in range(H):
        g_re = bb_re[hp:hp + 1, :] * cr - bb_im[hp:hp + 1, :] * ci
        g_im = bb_re[hp:hp + 1, :] * ci + bb_im[hp:hp + 1, :] * cr
        g_rows.append(jnp.where(lo_lane, g_re, -g_im))
    g_mat = jnp.concatenate(g_rows, axis=0)
    lag = lax.broadcasted_iota(jnp.int32, (1, L), 1).astype(F32)
    p_re, p_im = _s5_pow(step, lr_c, li_c, lag)
    k_scr[...] = jnp.dot(g_mat, jnp.where(lo_sub, p_re, p_im), precision=HI,
                         preferred_element_type=F32)

    srow = lax.broadcasted_iota(jnp.int32, (L, L), 0)
    tcol = lax.broadcasted_iota(jnp.int32, (L, L), 1)
    keep = tcol >= srow

    for h in range(H):
        for hp in range(H):
            kv = jnp.broadcast_to(k_scr[hp * H + h:hp * H + h + 1, :], (L, L))
            blk = pltpu.roll(kv, 0, 1, stride=1, stride_axis=0)
            toep[hp * L:(hp + 1) * L, h * L:(h + 1) * L] = jnp.where(keep, blk, 0.0).astype(BF16)

    s_col = lax.broadcasted_iota(jnp.int32, (L, 1), 0).astype(F32)
    q_re, q_im = _s5_pow(step, lr, li, (L - 1.0) - s_col)
    for hp in range(H):
        bin_scr[hp * L:(hp + 1) * L, :] = (q_re * v1[hp:hp + 1, :] + q_im * v2[hp:hp + 1, :]).astype(BF16)

    r_re, r_im = _s5_pow(step, lr_c, li_c, lag + 1.0)
    crt, cit = ct_ref[0, 0], ct_ref[0, 1]
    for h in range(H):
        c_r, c_i = crt[:, h:h + 1], cit[:, h:h + 1]
        cout[:, h * L:(h + 1) * L] = jnp.where(lo_sub, c_r * r_re - c_i * r_im,
                                               -(c_r * r_im + c_i * r_re)).astype(BF16)

    u = jnp.concatenate([u_ref[hp] for hp in range(H)], axis=-1)
    s_scr[...] = jnp.dot(u, bin_scr[...], preferred_element_type=F32)

    al_re, al_im = _s5_pow(step, lr, li, float(L))
    a1 = jnp.broadcast_to(al_re, (bsz, 2 * P))
    a2 = jnp.broadcast_to(jnp.where(lo_lane, -al_im, al_im), (bsz, 2 * P))

    x = xs = jnp.zeros((bsz, 2 * P), F32)
    for cidx in range(nchunk):
        sl = pl.ds(cidx, bsz, stride=nchunk)
        xin_scr[sl, :] = x
        s = s_scr[sl, :]
        x, xs = a1 * x + a2 * xs + s, a1 * xs - a2 * x + pltpu.roll(s, P, 1)

    y = jnp.dot(u, toep[...], preferred_element_type=F32)
    y = y + jnp.dot(xin_scr[...].astype(BF16), cout[...], preferred_element_type=F32)
    for h in range(H):
        y_ref[h] = y[:, h * L:(h + 1) * L].astype(BF16)


def _s5(ubt, ls, lrow, lcol, bt, cn, ct, bsz, seq):
    t = ubt.shape[1]
    rows = t // S5_L
    u3 = ubt.reshape(ubt.shape[0], rows, S5_L)
    P, H = S5_STATE, S5_GROUP
    g4 = lambda a, b: pl.BlockSpec((1, 2, a, b), lambda g: (g, 0, 0, 0))
    y3 = pl.pallas_call(
        functools.partial(_s5_kernel, bsz=bsz),
        grid=(S5_GROUPS,),
        in_specs=[pl.BlockSpec((1, 1, 1), lambda g: (g, 0, 0)),
                  pl.BlockSpec((1, 2, 2 * P), lambda g: (g, 0, 0)),
                  pl.BlockSpec((1, 2 * P, 2), lambda g: (g, 0, 0)),
                  g4(H, 2 * P), g4(H, 2 * P), g4(2 * P, H),
                  pl.BlockSpec((H, rows, S5_L), lambda g: (g, 0, 0))],
        out_specs=pl.BlockSpec((H, rows, S5_L), lambda g: (g, 0, 0)),
        out_shape=jax.ShapeDtypeStruct((S5_WIDTH, rows, S5_L), BF16),
        scratch_shapes=[pltpu.VMEM((H * H, S5_L), F32),
                        pltpu.VMEM((S5_K, S5_K), BF16),
                        pltpu.VMEM((S5_K, 2 * P), BF16),
                        pltpu.VMEM((2 * P, S5_K), BF16),
                        pltpu.VMEM((rows, 2 * P), F32),
                        pltpu.VMEM((rows, 2 * P), F32)],
        compiler_params=_cparams(("parallel",)),
        name="s5",
    )(ls, lrow, lcol, bt, cn, ct, u3)
    return y3.reshape(S5_WIDTH, t)


def _gelu_tanh(x):
    return 0.5 * x * (1.0 + jnp.tanh(0.7978845608028654 * (x + 0.044715 * (x * x * x))))


def _s5_post_kernel(y_ref, u_ref, z_ref, d_ref, wt_ref, b_ref, o_ref):
    y = y_ref[...].astype(F32) + d_ref[...] * u_ref[...].astype(F32)
    gl = _gelu_tanh(y)
    gate = jnp.dot(wt_ref[...], gl.astype(BF16), preferred_element_type=F32) + b_ref[...]
    out = gl * jax.nn.sigmoid(gate) * _silu(z_ref[...].astype(F32))
    o_ref[...] = out.T.astype(BF16)


def _s5_post(y_t, ubt, d_col, w_sq, layer, b_col, tn):
    t = y_t.shape[1]
    colv = pl.BlockSpec((1024, 1), lambda i: (0, 0))
    return pl.pallas_call(
        _s5_post_kernel,
        grid=(t // tn,),
        in_specs=[pl.BlockSpec((1024, tn), lambda i: (0, i)),
                  pl.BlockSpec((1024, tn), lambda i: (0, i)),
                  pl.BlockSpec((1024, tn), lambda i: (1, i)),
                  colv,
                  pl.BlockSpec((None, None, 1024, 1024), lambda i: (layer, SQ_GLU_T, 0, 0)),
                  colv],
        out_specs=pl.BlockSpec((tn, 1024), lambda i: (i, 0)),
        out_shape=jax.ShapeDtypeStruct((t, S5_WIDTH), BF16),
        compiler_params=_cparams(("parallel",)),
        name="s5_post",
    )(y_t, ubt, ubt, d_col, w_sq, b_col)


def _rope_tables_kernel(f_ref, fcol_ref, cos_ref, sa_ref, sb_ref, cost_ref, sint_ref):
    ts = cos_ref.shape[0]
    base = pl.program_id(0) * ts
    pos = (base + lax.broadcasted_iota(jnp.int32, (ts, 1), 0)).astype(F32)
    ang = pos * f_ref[...]
    cos, sin = jnp.cos(ang), jnp.sin(ang)
    lane = lax.broadcasted_iota(jnp.int32, (1, HEAD_PAD), 1)
    half = MLA_ROPE // 2
    lo = (lane >= ROPE_LO) & (lane < ROPE_LO + half)
    hi = (lane >= ROPE_LO + half) & (lane < ROPE_LO + MLA_ROPE)
    cos_ref[...] = jnp.where(lo | hi, cos, 0.0)
    sa_ref[...] = jnp.where(lo, -sin, 0.0)
    sb_ref[...] = jnp.where(hi, sin, 0.0)
    pos_t = (base + lax.broadcasted_iota(jnp.int32, (1, ts), 1)).astype(F32)
    ang_t = fcol_ref[...] * pos_t
    cost_ref[...] = jnp.cos(ang_t)
    sint_ref[...] = jnp.sin(ang_t)


def _rope_tables(freq_lane, freq_col, seq, ts):
    half = MLA_ROPE // 2
    out = jax.ShapeDtypeStruct((seq, HEAD_PAD), F32)
    out_t = jax.ShapeDtypeStruct((half, seq), F32)
    blk = pl.BlockSpec((ts, HEAD_PAD), lambda i: (i, 0))
    blk_t = pl.BlockSpec((half, ts), lambda i: (0, i))
    return pl.pallas_call(
        _rope_tables_kernel,
        grid=(seq // ts,),
        in_specs=[pl.BlockSpec((1, HEAD_PAD), lambda i: (0, 0)),
                  pl.BlockSpec((half, 1), lambda i: (0, 0))],
        out_specs=[blk, blk, blk, blk_t, blk_t],
        out_shape=[out, out, out, out_t, out_t],
        compiler_params=_cparams(("parallel",)),
        name="rope_tables",
    )(freq_lane, freq_col)


def _rope(x, cos, sa, sb):
    half = MLA_ROPE // 2
    return (x * cos + pltpu.roll(x, HEAD_PAD - half, 1) * sa + pltpu.roll(x, half, 1) * sb)


ATT_T = 512
ATT_HP = 2
V_ROWS = MLA_V + BF16_ROWS
LOG2E = 1.4426950408889634


def _mla_prep_kernel(cq_ref, ckv_ref, sm_ref, qn_ref, kvn_ref, wqt_ref, wk_ref, wvt_ref,
                     cos_ref, sa_ref, sb_ref, cost_ref, sint_ref, qt_ref, k_ref, vt_ref):
    tm = cq_ref.shape[0]
    half = MLA_ROPE // 2
    scale = (MLA_NOPE + MLA_ROPE) ** -0.5 * LOG2E
    qn = _rms(cq_ref[...], qn_ref[...]).astype(BF16)
    q_t = lax.dot_general(wqt_ref[...], qn, NT, preferred_element_type=F32)
    c_t, s_t = cost_ref[...], sint_ref[...]
    for h in range(MLA_HEADS):
        r0 = h * HEAD_PAD
        r1 = r0 + ROPE_LO
        x1, x2 = q_t[r1:r1 + half, :], q_t[r1 + half:r1 + 2 * half, :]
        pieces = [(r0, (q_t[r0:r1, :] * scale).astype(BF16)),
                  (r1, ((x1 * c_t - x2 * s_t) * scale).astype(BF16)),
                  (r1 + half, ((x1 * s_t + x2 * c_t) * scale).astype(BF16)),
                  (r1 + 2 * half, jnp.zeros((HEAD_PAD - ROPE_LO - 2 * half, tm), BF16))]
        for j in range(tm // ATT_T):
            for row, val in pieces:
                qt_ref[j, row:row + val.shape[0], :] = val[:, j * ATT_T:(j + 1) * ATT_T]

    kvn = _rms(ckv_ref[...], kvn_ref[...]).astype(BF16)
    k = jnp.dot(kvn, wk_ref[...], preferred_element_type=F32)
    kr = _rope(sm_ref[...], cos_ref[...], sa_ref[...], sb_ref[...])
    for h in range(MLA_HEADS):
        sl = slice(h * HEAD_PAD, (h + 1) * HEAD_PAD)
        k_ref[:, sl] = (k[:, sl] + kr).astype(BF16)
    v_t = lax.dot_general(wvt_ref[...], kvn, NT, preferred_element_type=F32)
    ones_row = (lax.broadcasted_iota(jnp.int32, (BF16_ROWS, ATT_T), 0) == 0).astype(BF16)
    for j in range(tm // ATT_T):
        for h in range(MLA_HEADS):
            vt_ref[j, h * V_ROWS:h * V_ROWS + MLA_V, :] = (
                v_t[h * MLA_V:(h + 1) * MLA_V, j * ATT_T:(j + 1) * ATT_T].astype(BF16))
            vt_ref[j, h * V_ROWS + MLA_V:(h + 1) * V_ROWS, :] = ones_row


def _mla_prep(tail, qn, kvn, wqt, wk, wvt, tables, seq, tm):
    cos, sa, sb, cost, sint = tables
    t = tail.shape[0]
    npos = seq // tm
    half = MLA_ROPE // 2
    tab = pl.BlockSpec((tm, HEAD_PAD), lambda i: (i % npos, 0))
    tab_t = pl.BlockSpec((half, tm), lambda i: (0, i % npos))
    full = lambda a, b: pl.BlockSpec((a, b), lambda i: (0, 0))
    hw = MLA_HEADS * HEAD_PAD
    nkv = tm // ATT_T
    return pl.pallas_call(
        _mla_prep_kernel,
        grid=(t // tm,),
        in_specs=[pl.BlockSpec((tm, MLA_Q_RANK), lambda i: (i, TAIL_CQ // MLA_Q_RANK)),
                  pl.BlockSpec((tm, MLA_KV_RANK), lambda i: (i, TAIL_CKV // MLA_KV_RANK)),
                  pl.BlockSpec((tm, LANES), lambda i: (i, TAIL_SMALL // LANES)),
                  full(1, MLA_Q_RANK), full(1, MLA_KV_RANK),
                  full(hw, MLA_Q_RANK), full(MLA_KV_RANK, hw), full(MLA_WIDTH, MLA_KV_RANK),
                  tab, tab, tab, tab_t, tab_t],
        out_specs=[pl.BlockSpec((nkv, hw, ATT_T), lambda i: (i, 0, 0)),
                   pl.BlockSpec((tm, hw), lambda i: (i, 0)),
                   pl.BlockSpec((nkv, MLA_HEADS * V_ROWS, ATT_T), lambda i: (i, 0, 0))],
        out_shape=[jax.ShapeDtypeStruct((t // ATT_T, hw, ATT_T), BF16),
                   jax.ShapeDtypeStruct((t, hw), BF16),
                   jax.ShapeDtypeStruct((t // ATT_T, MLA_HEADS * V_ROWS, ATT_T), BF16)],
        compiler_params=_cparams(("parallel",)),
        name="mla_prep",
    )(tail, tail, tail, qn, kvn, wqt, wk, wvt, cos, sa, sb, cost, sint)


ATT_QT = 4


def _attn_kernel(qt_ref, k_ref, vt_ref, o_ref, acc_scr, m_scr, s0, s1, mb0, mb1):
    t = ATT_T
    nq = qt_ref.shape[0]
    npos = 2 * nq + 2
    a = 2 * pl.program_id(2)
    tiles = (a, nq - 1 - a, a + 1, nq - 2 - a)
    m_scr[...] = jnp.full(m_scr.shape, MASK_VALUE, F32)
    acc_scr[...] = jnp.zeros(acc_scr.shape, F32)

    def locate(pos):
        if isinstance(pos, int) and pos < ATT_QT:
            return pos, tiles[pos], tiles[pos]
        r = pos - ATT_QT
        c0, c1, c2 = tiles[0], tiles[0] + tiles[1], tiles[0] + tiles[1] + tiles[2]
        slot = (r >= c0).astype(jnp.int32) + (r >= c1).astype(jnp.int32) + (r >= c2).astype(jnp.int32)
        start = jnp.where(slot == 0, 0, jnp.where(slot == 1, c0, jnp.where(slot == 2, c1, c2)))
        qtile = jnp.where(slot == 0, tiles[0], jnp.where(slot == 1, tiles[1],
                                                         jnp.where(slot == 2, tiles[2], tiles[3])))
        return slot, qtile, r - start

    def scores(pos, s_buf, mb_buf):
        _, qtile, kidx = locate(pos)
        masked = isinstance(pos, int) and pos < ATT_QT
        off = pl.multiple_of(kidx * t, t)
        for hh in range(ATT_HP):
            q_t = qt_ref[qtile, hh * HEAD_PAD:(hh + 1) * HEAD_PAD, :]
            k = k_ref[pl.ds(off, t), hh * HEAD_PAD:(hh + 1) * HEAD_PAD]
            s = jnp.dot(k, q_t, preferred_element_type=F32)
            if masked:
                kv_i = lax.broadcasted_iota(jnp.int32, (t, t), 0)
                q_i = lax.broadcasted_iota(jnp.int32, (t, t), 1)
                s = jnp.where(kv_i <= q_i, s, MASK_VALUE)
            s_buf[hh] = s
            mb_buf[hh:hh + 1, :] = jnp.max(s, axis=0, keepdims=True)

    def update(pos, s_buf, mb_buf):
        slot, _, kidx = locate(pos)
        for hh in range(ATT_HP):
            rows = slice(hh * V_ROWS, (hh + 1) * V_ROWS)
            m_old = m_scr[slot, hh:hh + 1, :]
            m_new = jnp.maximum(m_old, mb_buf[hh:hh + 1, :])
            alpha = jnp.exp2(m_old - m_new)
            p = jnp.exp2(s_buf[hh] - m_new).astype(BF16)
            m_scr[slot, hh:hh + 1, :] = m_new
            pv = jnp.dot(vt_ref[kidx, rows, :], p, preferred_element_type=F32)
            acc_scr[slot, rows, :] = alpha * acc_scr[slot, rows, :] + pv

    def pair(i):
        scores(2 * i + 1, s1, mb1)
        update(2 * i, s0, mb0)
        scores(2 * i + 2, s0, mb0)
        update(2 * i + 1, s1, mb1)

    scores(0, s0, mb0)
    for i in range(ATT_QT // 2):
        pair(i)

    first = ATT_QT // 2
    n_pairs = npos // 2 - 1 - first
    unroll = max(u for u in (7, 2, 1) if n_pairs % u == 0)

    def body(j, carry):
        for u in range(unroll):
            pair(first + unroll * j + u)
        return carry

    lax.fori_loop(0, n_pairs // unroll, body, 0)
    scores(npos - 1, s1, mb1)
    update(npos - 2, s0, mb0)
    update(npos - 1, s1, mb1)

    for slot in range(ATT_QT):
        outs = []
        for hh in range(ATT_HP):
            r0 = hh * V_ROWS
            outs.append(acc_scr[slot, r0:r0 + MLA_V, :] / acc_scr[slot, r0 + MLA_V:r0 + MLA_V + 1, :])
        off = pl.multiple_of(tiles[slot] * t, t)
        o_ref[pl.ds(off, t), :] = jnp.concatenate(outs, axis=0).T.astype(BF16)


def _mla_attn(q_t, k, v_t, bsz, seq):
    nq = seq // ATT_T
    nhp = MLA_HEADS // ATT_HP
    assert nq % ATT_QT == 0
    return pl.pallas_call(
        _attn_kernel,
        grid=(bsz, nhp, nq // ATT_QT),
        in_specs=[pl.BlockSpec((nq, ATT_HP * HEAD_PAD, ATT_T), lambda b, h, i: (b, h, 0)),
                  pl.BlockSpec((seq, ATT_HP * HEAD_PAD), lambda b, h, i: (b, h)),
                  pl.BlockSpec((nq, ATT_HP * V_ROWS, ATT_T), lambda b, h, i: (b, h, 0))],
        out_specs=pl.BlockSpec((seq, ATT_HP * MLA_V), lambda b, h, i: (b, h)),
        out_shape=jax.ShapeDtypeStruct((bsz * seq, MLA_WIDTH), BF16),
        scratch_shapes=[pltpu.VMEM((ATT_QT, ATT_HP * V_ROWS, ATT_T), F32),
                        pltpu.VMEM((ATT_QT, 8, ATT_T), F32),
                        pltpu.VMEM((ATT_HP, ATT_T, ATT_T), F32),
                        pltpu.VMEM((ATT_HP, ATT_T, ATT_T), F32),
                        pltpu.VMEM((8, ATT_T), F32),
                        pltpu.VMEM((8, ATT_T), F32)],
        compiler_params=_cparams(("parallel", "parallel", "arbitrary")),
        name="mla_attn",
    )(q_t, k, v_t)


def _merge_kernel(x_ref, ya_ref, yb_ref, o_ref, zc_ref, g0_ref, g1_ref, g2_ref,
                  wa_ref, wb_ref, wc_ref, wo_ref, pn_ref, out_ref):
    dot = lambda a, w: jnp.dot(a, w[...], preferred_element_type=F32)
    sig = lambda r: jax.nn.sigmoid(r[...].astype(F32))
    yc = (o_ref[...].astype(F32) * _silu(zc_ref[...].astype(F32))).astype(BF16)
    merged = (sig(g0_ref) * dot(ya_ref[...], wa_ref)
              + sig(g1_ref) * dot(yb_ref[...], wb_ref)
              + sig(g2_ref) * dot(yc, wc_ref))
    out = dot(merged.astype(BF16), wo_ref)
    out_ref[...] = x_ref[...] + _rms(out, pn_ref[...])


SQ_A, SQ_B, SQ_C, SQ_O, SQ_GLU_T = range(5)


def _merge(x2d, ya, yb, o, main, w_sq, layer, pn, tm):
    t = x2d.shape[0]
    col = lambda j: pl.BlockSpec((tm, 1024), lambda i: (i, j))
    wsel = lambda k: pl.BlockSpec((None, None, 1024, 1024), lambda i: (layer, k, 0, 0))
    return pl.pallas_call(
        _merge_kernel,
        grid=(t // tm,),
        in_specs=[col(0), col(0), col(0), col(0), col(MAIN_ZC), col(MAIN_G0), col(MAIN_G0 + 1),
                  col(MAIN_G0 + 2), wsel(SQ_A), wsel(SQ_B), wsel(SQ_C), wsel(SQ_O),
                  pl.BlockSpec((1, 1024), lambda i: (0, 0))],
        out_specs=col(0),
        out_shape=jax.ShapeDtypeStruct((t, D_MODEL), F32),
        compiler_params=_cparams(("parallel",)),
        name="merge",
    )(x2d, ya, yb, o, main, main, main, main, w_sq, w_sq, w_sq, w_sq, pn)


def _pack_w_in(w):
    sizes = (SSD_D_INNER, SSD_D_INNER + 2 * SSD_GROUPS * SSD_STATE, SSD_HEADS, S5_WIDTH, S5_WIDTH,
             MLA_Q_RANK, MLA_KV_RANK, MLA_ROPE, MLA_WIDTH, 3 * D_MODEL)
    parts, start = [], 0
    for n in sizes:
        parts.append(w[..., start:start + n])
        start += n
    z_a, xbc, dt, u_b, z_b, c_q, c_kv, k_rope, z_c, gates = parts
    zeros = lambda n: jnp.zeros(w.shape[:-1] + (n,), w.dtype)
    small = jnp.concatenate([dt, zeros(ROPE_LO - SSD_HEADS), k_rope,
                             zeros(LANES - ROPE_LO - MLA_ROPE)], axis=-1)
    tail = jnp.concatenate([c_q, c_kv, small, zeros(1024 - TAIL_SMALL - LANES)], axis=-1)
    w_cat = jnp.concatenate([z_a, xbc, z_c, gates, tail], axis=-1).astype(BF16)
    w_t = jnp.swapaxes(jnp.concatenate([u_b, z_b], axis=-1), -1, -2).astype(BF16)
    return w_cat, w_t


def _pad_heads(w, lo, width):
    r = w.shape[0]
    w3 = w.reshape(r, MLA_HEADS, -1)[:, :, lo:lo + width]
    w3 = jnp.pad(w3, ((0, 0), (0, 0), (0, HEAD_PAD - width)))
    return w3.reshape(r, MLA_HEADS * HEAD_PAD)


def _row(v, n=None):
    v = v.reshape(1, -1).astype(F32)
    if n is not None and v.shape[1] < n:
        v = jnp.pad(v, ((0, 0), (0, n - v.shape[1])))
    return v


def _dup(a, axis):
    return jnp.concatenate([a, a], axis=axis).astype(F32)


def kernel(x, pre_norm, w_in, conv_w, conv_b, dt_bias, a_log, d_ssd, ssd_norm, w_a, s5_log_step, s5_lambda_re, s5_lambda_im, s5_b_re, s5_b_im, s5_c_re, s5_c_im, s5_d, w_glu, b_glu, w_b, q_norm, w_uq, kv_norm, w_ukv, w_c, w_o, post_norm):
    bsz, seq, _ = x.shape
    depth = w_in.shape[0]
    t = bsz * seq
    assert seq % ATT_T == 0 and seq % SSD_CHUNK == 0 and seq % S5_L == 0
    tm = min(1024, seq)

    half = MLA_ROPE // 2
    inv_freq = ROPE_THETA ** (-jnp.arange(0, MLA_ROPE, 2, dtype=F32) / MLA_ROPE)
    freq_lane = jnp.concatenate([jnp.zeros((ROPE_LO,), F32), inv_freq, inv_freq,
                                 jnp.zeros((HEAD_PAD - ROPE_LO - 2 * half,), F32)]).reshape(1, HEAD_PAD)
    tables = _rope_tables(freq_lane, inv_freq.reshape(half, 1), seq, min(512, seq))

    w_cat, w_t = _pack_w_in(w_in)
    w_sq = jnp.stack([w_a, w_b, w_c, w_o, jnp.swapaxes(w_glu, 1, 2)], axis=1).astype(BF16)

    x2d = x.reshape(t, D_MODEL)
    for l in range(depth):
        main, tail, ubt = _inproj(x2d, _row(pre_norm[l]), w_cat, w_t, l, tm)

        dsk = jnp.repeat(d_ssd[l].astype(F32), SSD_HEAD_DIM).reshape(1, SSD_D_INNER)
        y_a = _ssd(main, tail, conv_w[l].astype(BF16), _row(conv_b[l]), _row(dt_bias[l], LANES),
                   _row(a_log[l], LANES), dsk, _row(ssd_norm[l]), bsz, seq)

        ls3 = s5_log_step[l].astype(F32).reshape(S5_GROUPS, 1, 1)
        lam = jnp.stack([s5_lambda_re[l], s5_lambda_im[l]], axis=1)
        lrow = _dup(lam, 2)
        lcol = jnp.swapaxes(lrow, 1, 2)
        bt = _dup(jnp.stack([jnp.swapaxes(s5_b_re[l], 1, 2),
                             jnp.swapaxes(s5_b_im[l], 1, 2)], axis=1), 3)
        cn = _dup(jnp.stack([s5_c_re[l], s5_c_im[l]], axis=1), 3)
        ct = jnp.swapaxes(cn, 2, 3)
        y_t = _s5(ubt, ls3, lrow, lcol, bt, cn, ct, bsz, seq)
        y_b = _s5_post(y_t, ubt, s5_d[l].astype(F32).reshape(S5_WIDTH, 1), w_sq, l,
                       b_glu[l].astype(F32).reshape(S5_WIDTH, 1), min(512, seq))

        stride_q = MLA_NOPE + MLA_ROPE
        wqt = _pad_heads(w_uq[l], 0, stride_q).T.astype(BF16)
        wk = _pad_heads(w_ukv[l], 0, MLA_NOPE).astype(BF16)
        wvt = (w_ukv[l].reshape(MLA_KV_RANK, MLA_HEADS, MLA_NOPE + MLA_V)[:, :, MLA_NOPE:]
               .reshape(MLA_KV_RANK, MLA_WIDTH).T.astype(BF16))
        q_t, k, v_t = _mla_prep(tail, _row(q_norm[l]), _row(kv_norm[l]), wqt, wk, wvt,
                                tables, seq, tm)
        o = _mla_attn(q_t, k, v_t, bsz, seq)

        x2d = _merge(x2d, y_a, y_b, o, main, w_sq, l, _row(post_norm[l]), min(512, seq))
    return x2d.reshape(bsz, seq, D_MODEL)
```

```python
import functools

import numpy as np

import jax
import jax.numpy as jnp
from jax import lax
from jax.experimental import pallas as pl
from jax.experimental.pallas import tpu as pltpu

F32 = jnp.float32
BF16 = jnp.bfloat16

D_MODEL = 1024
SSD_D_INNER = 1024
SSD_HEAD_DIM = 64
SSD_HEADS = 16
SSD_GROUPS = 4
SSD_STATE = 128
SSD_CONV = 4
SSD_CHUNK = 128
S5_WIDTH = 1024
S5_GROUP = 16
S5_GROUPS = 64
S5_STATE = 64
MLA_HEADS = 16
MLA_NOPE = 64
MLA_ROPE = 32
MLA_V = 64
MLA_Q_RANK = 512
MLA_KV_RANK = 256
MLA_WIDTH = 1024
ROPE_THETA = 10000.0
EPS = 1e-6
MASK_VALUE = -1e30

LANES = 128
BF16_ROWS = 16
S5_L = LANES
S5_K = S5_L * S5_GROUP
HEAD_PAD = 128
ROPE_LO = MLA_NOPE
VMEM_LIMIT = 56 * 1024 * 1024
HI = lax.Precision.HIGHEST
NT = (((1,), (1,)), ((), ()))


def _cparams(sem):
    return pltpu.CompilerParams(dimension_semantics=sem, vmem_limit_bytes=VMEM_LIMIT)


def _silu(x):
    h = 0.5 * x
    return h * jnp.tanh(h) + h


def _rms(x, g):
    return x * lax.rsqrt(jnp.mean(x * x, axis=-1, keepdims=True) + EPS) * g


N_MAIN = 7
N_TRANS = 2
MAIN_ZC, MAIN_G0 = 3, 4
TAIL_CQ, TAIL_CKV, TAIL_SMALL = 0, 512, 768


def _inproj_kernel(x_ref, g_ref, w_ref, wt_ref, main_ref, tail_ref, ubt_ref, h_scr):
    j = pl.program_id(1)

    @pl.when(j == 0)
    def _():
        h_scr[...] = _rms(x_ref[...], g_ref[...]).astype(BF16)

    @pl.when(j < N_MAIN)
    def _():
        main_ref[...] = jnp.dot(h_scr[...], w_ref[...], preferred_element_type=F32).astype(BF16)

    @pl.when(j == N_MAIN)
    def _():
        tail_ref[...] = jnp.dot(h_scr[...], w_ref[...], preferred_element_type=F32)

    @pl.when(j > N_MAIN)
    def _():
        ubt_ref[...] = lax.dot_general(wt_ref[...], h_scr[...], NT,
                                       preferred_element_type=F32).astype(BF16)


def _inproj(x2d, gain, w_cat, w_t, layer, tm):
    t = x2d.shape[0]
    tblk = lambda j: jnp.clip(j - (N_MAIN + 1), 0, N_TRANS - 1)
    return pl.pallas_call(
        _inproj_kernel,
        grid=(t // tm, N_MAIN + 1 + N_TRANS),
        in_specs=[
            pl.BlockSpec((tm, D_MODEL), lambda i, j: (i, 0)),
            pl.BlockSpec((1, D_MODEL), lambda i, j: (0, 0)),
            pl.BlockSpec((None, D_MODEL, 1024), lambda i, j: (layer, 0, jnp.minimum(j, N_MAIN))),
            pl.BlockSpec((None, 1024, D_MODEL), lambda i, j: (layer, tblk(j), 0)),
        ],
        out_specs=[
            pl.BlockSpec((tm, 1024), lambda i, j: (i, jnp.minimum(j, N_MAIN - 1))),
            pl.BlockSpec((tm, 1024), lambda i, j: (i, 0)),
            pl.BlockSpec((1024, tm), lambda i, j: (tblk(j), i)),
        ],
        out_shape=[
            jax.ShapeDtypeStruct((t, N_MAIN * 1024), BF16),
            jax.ShapeDtypeStruct((t, 1024), F32),
            jax.ShapeDtypeStruct((N_TRANS * 1024, t), BF16),
        ],
        scratch_shapes=[pltpu.VMEM((tm, D_MODEL), BF16)],
        compiler_params=_cparams(("parallel", "arbitrary")),
        name="inproj",
    )(x2d, gain, w_cat, w_t)


SSD_PAD = BF16_ROWS
HEADS_PER_GROUP = SSD_HEADS // SSD_GROUPS


def _ssd_constants():
    L = SSD_CHUNK
    win = SSD_PAD + L
    shift = np.zeros((L, SSD_CONV * win), np.float32)
    for k in range(SSD_CONV):
        for t in range(L):
            shift[t, k * win + SSD_PAD + t - (SSD_CONV - 1) + k] = 1.0
    e_wide = np.zeros((3 * LANES, SSD_HEADS * LANES), np.float32)
    for part in range(3):
        for h in range(SSD_HEADS):
            e_wide[part * LANES + h, h * LANES:(h + 1) * LANES] = 1.0
    return jnp.asarray(shift, BF16), jnp.asarray(e_wide, BF16)


def _split3(x):
    hi = x.astype(BF16)
    r1 = x - hi.astype(F32)
    mid = r1.astype(BF16)
    lo = (r1 - mid.astype(F32)).astype(BF16)
    return jnp.concatenate([hi, mid, lo], axis=-1)


SSD_STEP_CHUNKS = 4


def _ssd_kernel(za_ref, xs_ref, bc_ref, sm_ref, cw_ref, cb_ref, dtb_ref, alog_ref,
                dsk_ref, ng_ref, shift_ref, ewide_ref, y_ref, xext, state):
    L = SSD_CHUNK
    rows = SSD_STEP_CHUNKS * L

    @pl.when(pl.program_id(1) == 0)
    def _():
        xext[0:SSD_PAD, :] = jnp.zeros((SSD_PAD, 2048), BF16)
        state[...] = jnp.zeros(state.shape, F32)

    xext[SSD_PAD:SSD_PAD + rows, 0:1024] = xs_ref[...]
    xext[SSD_PAD:SSD_PAD + rows, 1024:2048] = bc_ref[...]
    for ci in range(SSD_STEP_CHUNKS):
        _ssd_chunk(ci, za_ref, sm_ref, cw_ref, cb_ref, dtb_ref, alog_ref, dsk_ref, ng_ref,
                   shift_ref, ewide_ref, y_ref, xext, state)
    xext[0:SSD_PAD, :] = xext[rows:rows + SSD_PAD, :]


def _ssd_chunk(ci, za_ref, sm_ref, cw_ref, cb_ref, dtb_ref, alog_ref, dsk_ref, ng_ref,
               shift_ref, ewide_ref, y_ref, xext, state):
    L = SSD_CHUNK
    r0 = ci * L
    window = xext[r0:r0 + SSD_PAD + L, :]
    taps = jnp.concatenate([window * cw_ref[k:k + 1, :] for k in range(SSD_CONV)], axis=0)
    acc = jnp.dot(shift_ref[...], taps, preferred_element_type=F32) + cb_ref[...]
    xbc = _silu(acc)

    lane = lax.broadcasted_iota(jnp.int32, (1, LANES), 1)
    dt = jax.nn.softplus(sm_ref[r0:r0 + L, :] + dtb_ref[...])
    a = jnp.where(lane < SSD_HEADS, -jnp.exp(alog_ref[...]), 0.0)
    adt = dt * a
    row_i = lax.broadcasted_iota(jnp.int32, (L, L), 0)
    col_i = lax.broadcasted_iota(jnp.int32, (L, L), 1)
    causal = row_i >= col_i
    tri = causal.astype(BF16)
    adt3 = _split3(adt)
    cs = jnp.dot(jnp.concatenate([tri, tri, tri], axis=1),
                 jnp.concatenate([adt3[:, 0:LANES], adt3[:, LANES:2 * LANES], adt3[:, 2 * LANES:]], axis=0),
                 preferred_element_type=F32)
    cs_t = cs.T
    dt_t = dt.T
    cs3 = _split3(cs)
    cs_wide = jnp.dot(cs3, ewide_ref[...], preferred_element_type=F32)

    xs_all = xbc[:, 0:SSD_D_INNER]
    xs16 = xs_all.astype(BF16)
    pair_lane = lax.broadcasted_iota(jnp.int32, (1, 2 * SSD_HEAD_DIM), 1) < SSD_HEAD_DIM
    y_groups = []
    for g in range(SSD_GROUPS):
        b_g = xbc[:, SSD_D_INNER + g * SSD_STATE:SSD_D_INNER + (g + 1) * SSD_STATE]
        c_g = xbc[:, SSD_D_INNER + (SSD_GROUPS + g) * SSD_STATE:
                  SSD_D_INNER + (SSD_GROUPS + g + 1) * SSD_STATE]
        c_g16 = c_g.astype(BF16)
        b_gt = b_g.T
        cb = jnp.dot(c_g16, b_gt.astype(BF16), preferred_element_type=F32)
        st = state[g]
        y_off = jnp.dot(c_g16, st.astype(BF16), preferred_element_type=F32)
        y_pairs, st_pairs = [], []
        for pr in range(HEADS_PER_GROUP // 2):
            lhs = []
            for r in range(2):
                h = g * HEADS_PER_GROUP + 2 * pr + r
                row = cs_t[h:h + 1, :]
                last = cs_t[h:h + 1, L - 1:L]
                dt_row = dt_t[h:h + 1, :]
                decay = jnp.exp(jnp.where(causal, cs_wide[:, h * LANES:(h + 1) * LANES] - row, MASK_VALUE))
                lhs.append((cb * decay * dt_row).astype(BF16))
                lhs.append((b_gt * (jnp.exp(last - row) * dt_row)).astype(BF16))
            h0 = g * HEADS_PER_GROUP + 2 * pr
            res = jnp.dot(jnp.concatenate(lhs, axis=0), xs16[:, h0 * SSD_HEAD_DIM:(h0 + 2) * SSD_HEAD_DIM],
                          preferred_element_type=F32)
            y_pairs.append(jnp.where(pair_lane, res[0:L], res[2 * L:3 * L]))
            st_pairs.append(jnp.where(pair_lane, res[L:2 * L], res[3 * L:4 * L]))
        wide = lambda h: cs_wide[:, h * LANES:(h + 1) * LANES]
        h_g = g * HEADS_PER_GROUP
        cs_g = jnp.concatenate([jnp.where(pair_lane, wide(h_g + 2 * pr), wide(h_g + 2 * pr + 1))
                                for pr in range(HEADS_PER_GROUP // 2)], axis=-1)
        y_groups.append(jnp.concatenate(y_pairs, axis=-1) + jnp.exp(cs_g) * y_off)
        state[g] = st * jnp.exp(cs_g[L - 1:L, :]) + jnp.concatenate(st_pairs, axis=-1)

    y = jnp.concatenate(y_groups, axis=-1) + dsk_ref[...] * xs_all
    y = y * _silu(za_ref[r0:r0 + L, :].astype(F32))
    y_ref[r0:r0 + L, :] = _rms(y, ng_ref[...]).astype(BF16)


def _ssd(main, tail, conv_w, conv_b, dtb, alog, dsk, ng, bsz, seq):
    L = SSD_STEP_CHUNKS * SSD_CHUNK
    assert seq % L == 0
    nc = seq // L
    shift, e_wide = _ssd_constants()
    row = lambda b, c: b * nc + c
    vec = lambda n: pl.BlockSpec((1, n), lambda b, c: (0, 0))
    full = lambda a: pl.BlockSpec(a.shape, lambda b, c: (0, 0))
    return pl.pallas_call(
        _ssd_kernel,
        grid=(bsz, nc),
        in_specs=[
            pl.BlockSpec((L, 1024), lambda b, c: (row(b, c), 0)),
            pl.BlockSpec((L, 1024), lambda b, c: (row(b, c), 1)),
            pl.BlockSpec((L, 1024), lambda b, c: (row(b, c), 2)),
            pl.BlockSpec((L, LANES), lambda b, c: (row(b, c), TAIL_SMALL // LANES)),
            pl.BlockSpec((SSD_CONV, 2048), lambda b, c: (0, 0)),
            vec(2048), vec(LANES), vec(LANES), vec(1024), vec(1024),
            full(shift), full(e_wide),
        ],
        out_specs=pl.BlockSpec((L, 1024), lambda b, c: (row(b, c), 0)),
        out_shape=jax.ShapeDtypeStruct((bsz * seq, SSD_D_INNER), BF16),
        scratch_shapes=[
            pltpu.VMEM((SSD_PAD + L, 2048), BF16),
            pltpu.VMEM((SSD_GROUPS, SSD_STATE, HEADS_PER_GROUP * SSD_HEAD_DIM), F32),
        ],
        compiler_params=_cparams(("parallel", "arbitrary")),
        name="ssd",
    )(main, main, main, tail, conv_w, conv_b, dtb, alog, dsk, ng, shift, e_wide)


def _s5_pow(step, lr, li, k):
    mag = jnp.exp(lr * step * k)
    ang = li * step * k
    return mag * jnp.cos(ang), mag * jnp.sin(ang)


def _s5_kernel(ls_ref, lrow_ref, lcol_ref, bt_ref, cn_ref, ct_ref, u_ref, y_ref,
               k_scr, toep, bin_scr, cout, s_scr, xin_scr, *, bsz):
    P, H, L = S5_STATE, S5_GROUP, S5_L
    rows = u_ref.shape[1]
    nchunk = rows // bsz
    lane = lax.broadcasted_iota(jnp.int32, (1, 2 * P), 1)
    lo_lane = lane < P
    sub = lax.broadcasted_iota(jnp.int32, (2 * P, 1), 0)
    lo_sub = sub < P

    step = jnp.exp(ls_ref[0])
    lr, li = lrow_ref[0, 0:1, :], lrow_ref[0, 1:2, :]
    lr_c, li_c = lcol_ref[0, :, 0:1], lcol_ref[0, :, 1:2]

    abar_re, abar_im = _s5_pow(step, lr, li, 1.0)
    den = lr * lr + li * li
    nr, ni = abar_re - 1.0, abar_im
    f_re = (nr * lr + ni * li) / den
    f_im = (ni * lr - nr * li) / den
    br, bi = bt_ref[0, 0], bt_ref[0, 1]
    bb_re = f_re * br - f_im * bi
    bb_im = f_re * bi + f_im * br
    v1 = jnp.where(lo_lane, bb_re, bb_im)
    v2 = jnp.where(lo_lane, -bb_im, bb_re)

    cr, ci = cn_ref[0, 0], cn_ref[0, 1]
    g_rows = []
    for hp in range(H):
        g_re = bb_re[hp:hp + 1, :] * cr - bb_im[hp:hp + 1, :] * ci
        g_im = bb_re[hp:hp + 1, :] * ci + bb_im[hp:hp + 1, :] * cr
        g_rows.append(jnp.where(lo_lane, g_re, -g_im))
    g_mat = jnp.concatenate(g_rows, axis=0)
    lag = lax.broadcasted_iota(jnp.int32, (1, L), 1).astype(F32)
    p_re, p_im = _s5_pow(step, lr_c, li_c, lag)
    k_scr[...] = jnp.dot(g_mat, jnp.where(lo_sub, p_re, p_im), precision=HI,
                         preferred_element_type=F32)

    srow = lax.broadcasted_iota(jnp.int32, (L, L), 0)
    tcol = lax.broadcasted_iota(jnp.int32, (L, L), 1)
    keep = tcol >= srow

    for h in range(H):
        for hp in range(H):
            kv = jnp.broadcast_to(k_scr[hp * H + h:hp * H + h + 1, :], (L, L))
            blk = pltpu.roll(kv, 0, 1, stride=1, stride_axis=0)
            toep[hp * L:(hp + 1) * L, h * L:(h + 1) * L] = jnp.where(keep, blk, 0.0).astype(BF16)

    s_col = lax.broadcasted_iota(jnp.int32, (L, 1), 0).astype(F32)
    q_re, q_im = _s5_pow(step, lr, li, (L - 1.0) - s_col)
    for hp in range(H):
        bin_scr[hp * L:(hp + 1) * L, :] = (q_re * v1[hp:hp + 1, :] + q_im * v2[hp:hp + 1, :]).astype(BF16)

    r_re, r_im = _s5_pow(step, lr_c, li_c, lag + 1.0)
    crt, cit = ct_ref[0, 0], ct_ref[0, 1]
    for h in range(H):
        c_r, c_i = crt[:, h:h + 1], cit[:, h:h + 1]
        cout[:, h * L:(h + 1) * L] = jnp.where(lo_sub, c_r * r_re - c_i * r_im,
                                               -(c_r * r_im + c_i * r_re)).astype(BF16)

    u = jnp.concatenate([u_ref[hp] for hp in range(H)], axis=-1)
    s_scr[...] = jnp.dot(u, bin_scr[...], preferred_element_type=F32)

    al_re, al_im = _s5_pow(step, lr, li, float(L))
    a1 = jnp.broadcast_to(al_re, (bsz, 2 * P))
    a2 = jnp.broadcast_to(jnp.where(lo_lane, -al_im, al_im), (bsz, 2 * P))

    x = xs = jnp.zeros((bsz, 2 * P), F32)
    for cidx in range(nchunk):
        sl = pl.ds(cidx, bsz, stride=nchunk)
        xin_scr[sl, :] = x
        s = s_scr[sl, :]
        x, xs = a1 * x + a2 * xs + s, a1 * xs - a2 * x + pltpu.roll(s, P, 1)

    y = jnp.dot(u, toep[...], preferred_element_type=F32)
    y = y + jnp.dot(xin_scr[...].astype(BF16), cout[...], preferred_element_type=F32)
    for h in range(H):
        y_ref[h] = y[:, h * L:(h + 1) * L].astype(BF16)


def _s5(ubt, ls, lrow, lcol, bt, cn, ct, bsz, seq):
    t = ubt.shape[1]
    rows = t // S5_L
    u3 = ubt.reshape(ubt.shape[0], rows, S5_L)
    P, H = S5_STATE, S5_GROUP
    g4 = lambda a, b: pl.BlockSpec((1, 2, a, b), lambda g: (g, 0, 0, 0))
    y3 = pl.pallas_call(
        functools.partial(_s5_kernel, bsz=bsz),
        grid=(S5_GROUPS,),
        in_specs=[pl.BlockSpec((1, 1, 1), lambda g: (g, 0, 0)),
                  pl.BlockSpec((1, 2, 2 * P), lambda g: (g, 0, 0)),
                  pl.BlockSpec((1, 2 * P, 2), lambda g: (g, 0, 0)),
                  g4(H, 2 * P), g4(H, 2 * P), g4(2 * P, H),
                  pl.BlockSpec((H, rows, S5_L), lambda g: (g, 0, 0))],
        out_specs=pl.BlockSpec((H, rows, S5_L), lambda g: (g, 0, 0)),
        out_shape=jax.ShapeDtypeStruct((S5_WIDTH, rows, S5_L), BF16),
        scratch_shapes=[pltpu.VMEM((H * H, S5_L), F32),
                        pltpu.VMEM((S5_K, S5_K), BF16),
                        pltpu.VMEM((S5_K, 2 * P), BF16),
                        pltpu.VMEM((2 * P, S5_K), BF16),
                        pltpu.VMEM((rows, 2 * P), F32),
                        pltpu.VMEM((rows, 2 * P), F32)],
        compiler_params=_cparams(("parallel",)),
        name="s5",
    )(ls, lrow, lcol, bt, cn, ct, u3)
    return y3.reshape(S5_WIDTH, t)


def _gelu_tanh(x):
    return 0.5 * x * (1.0 + jnp.tanh(0.7978845608028654 * (x + 0.044715 * (x * x * x))))


def _s5_post_kernel(y_ref, u_ref, z_ref, d_ref, wt_ref, b_ref, o_ref):
    y = y_ref[...].astype(F32) + d_ref[...] * u_ref[...].astype(F32)
    gl = _gelu_tanh(y)
    gate = jnp.dot(wt_ref[...], gl.astype(BF16), preferred_element_type=F32) + b_ref[...]
    out = gl * jax.nn.sigmoid(gate) * _silu(z_ref[...].astype(F32))
    o_ref[...] = out.T.astype(BF16)


def _s5_post(y_t, ubt, d_col, w_sq, layer, b_col, tn):
    t = y_t.shape[1]
    colv = pl.BlockSpec((1024, 1), lambda i: (0, 0))
    return pl.pallas_call(
        _s5_post_kernel,
        grid=(t // tn,),
        in_specs=[pl.BlockSpec((1024, tn), lambda i: (0, i)),
                  pl.BlockSpec((1024, tn), lambda i: (0, i)),
                  pl.BlockSpec((1024, tn), lambda i: (1, i)),
                  colv,
                  pl.BlockSpec((None, None, 1024, 1024), lambda i: (layer, SQ_GLU_T, 0, 0)),
                  colv],
        out_specs=pl.BlockSpec((tn, 1024), lambda i: (i, 0)),
        out_shape=jax.ShapeDtypeStruct((t, S5_WIDTH), BF16),
        compiler_params=_cparams(("parallel",)),
        name="s5_post",
    )(y_t, ubt, ubt, d_col, w_sq, b_col)


def _rope_tables_kernel(f_ref, fcol_ref, cos_ref, sa_ref, sb_ref, cost_ref, sint_ref):
    ts = cos_ref.shape[0]
    base = pl.program_id(0) * ts
    pos = (base + lax.broadcasted_iota(jnp.int32, (ts, 1), 0)).astype(F32)
    ang = pos * f_ref[...]
    cos, sin = jnp.cos(ang), jnp.sin(ang)
    lane = lax.broadcasted_iota(jnp.int32, (1, HEAD_PAD), 1)
    half = MLA_ROPE // 2
    lo = (lane >= ROPE_LO) & (lane < ROPE_LO + half)
    hi = (lane >= ROPE_LO + half) & (lane < ROPE_LO + MLA_ROPE)
    cos_ref[...] = jnp.where(lo | hi, cos, 0.0)
    sa_ref[...] = jnp.where(lo, -sin, 0.0)
    sb_ref[...] = jnp.where(hi, sin, 0.0)
    pos_t = (base + lax.broadcasted_iota(jnp.int32, (1, ts), 1)).astype(F32)
    ang_t = fcol_ref[...] * pos_t
    cost_ref[...] = jnp.cos(ang_t)
    sint_ref[...] = jnp.sin(ang_t)


def _rope_tables(freq_lane, freq_col, seq, ts):
    half = MLA_ROPE // 2
    out = jax.ShapeDtypeStruct((seq, HEAD_PAD), F32)
    out_t = jax.ShapeDtypeStruct((half, seq), F32)
    blk = pl.BlockSpec((ts, HEAD_PAD), lambda i: (i, 0))
    blk_t = pl.BlockSpec((half, ts), lambda i: (0, i))
    return pl.pallas_call(
        _rope_tables_kernel,
        grid=(seq // ts,),
        in_specs=[pl.BlockSpec((1, HEAD_PAD), lambda i: (0, 0)),
                  pl.BlockSpec((half, 1), lambda i: (0, 0))],
        out_specs=[blk, blk, blk, blk_t, blk_t],
        out_shape=[out, out, out, out_t, out_t],
        compiler_params=_cparams(("parallel",)),
        name="rope_tables",
    )(freq_lane, freq_col)


def _rope(x, cos, sa, sb):
    half = MLA_ROPE // 2
    return (x * cos + pltpu.roll(x, HEAD_PAD - half, 1) * sa + pltpu.roll(x, half, 1) * sb)


ATT_T = 512
ATT_HP = 2
V_ROWS = MLA_V + BF16_ROWS
LOG2E = 1.4426950408889634


def _mla_prep_kernel(cq_ref, ckv_ref, sm_ref, qn_ref, kvn_ref, wqt_ref, wk_ref, wvt_ref,
                     cos_ref, sa_ref, sb_ref, cost_ref, sint_ref, qt_ref, k_ref, vt_ref):
    tm = cq_ref.shape[0]
    half = MLA_ROPE // 2
    scale = (MLA_NOPE + MLA_ROPE) ** -0.5 * LOG2E
    qn = _rms(cq_ref[...], qn_ref[...]).astype(BF16)
    q_t = lax.dot_general(wqt_ref[...], qn, NT, preferred_element_type=F32)
    c_t, s_t = cost_ref[...], sint_ref[...]
    for h in range(MLA_HEADS):
        r0 = h * HEAD_PAD
        r1 = r0 + ROPE_LO
        x1, x2 = q_t[r1:r1 + half, :], q_t[r1 + half:r1 + 2 * half, :]
        pieces = [(r0, (q_t[r0:r1, :] * scale).astype(BF16)),
                  (r1, ((x1 * c_t - x2 * s_t) * scale).astype(BF16)),
                  (r1 + half, ((x1 * s_t + x2 * c_t) * scale).astype(BF16)),
                  (r1 + 2 * half, jnp.zeros((HEAD_PAD - ROPE_LO - 2 * half, tm), BF16))]
        for j in range(tm // ATT_T):
            for row, val in pieces:
                qt_ref[j, row:row + val.shape[0], :] = val[:, j * ATT_T:(j + 1) * ATT_T]

    kvn = _rms(ckv_ref[...], kvn_ref[...]).astype(BF16)
    k = jnp.dot(kvn, wk_ref[...], preferred_element_type=F32)
    kr = _rope(sm_ref[...], cos_ref[...], sa_ref[...], sb_ref[...])
    for h in range(MLA_HEADS):
        sl = slice(h * HEAD_PAD, (h + 1) * HEAD_PAD)
        k_ref[:, sl] = (k[:, sl] + kr).astype(BF16)
    v_t = lax.dot_general(wvt_ref[...], kvn, NT, preferred_element_type=F32)
    ones_row = (lax.broadcasted_iota(jnp.int32, (BF16_ROWS, ATT_T), 0) == 0).astype(BF16)
    for j in range(tm // ATT_T):
        for h in range(MLA_HEADS):
            vt_ref[j, h * V_ROWS:h * V_ROWS + MLA_V, :] = (
                v_t[h * MLA_V:(h + 1) * MLA_V, j * ATT_T:(j + 1) * ATT_T].astype(BF16))
            vt_ref[j, h * V_ROWS + MLA_V:(h + 1) * V_ROWS, :] = ones_row


def _mla_prep(tail, qn, kvn, wqt, wk, wvt, tables, seq, tm):
    cos, sa, sb, cost, sint = tables
    t = tail.shape[0]
    npos = seq // tm
    half = MLA_ROPE // 2
    tab = pl.BlockSpec((tm, HEAD_PAD), lambda i: (i % npos, 0))
    tab_t = pl.BlockSpec((half, tm), lambda i: (0, i % npos))
    full = lambda a, b: pl.BlockSpec((a, b), lambda i: (0, 0))
    hw = MLA_HEADS * HEAD_PAD
    nkv = tm // ATT_T
    return pl.pallas_call(
        _mla_prep_kernel,
        grid=(t // tm,),
        in_specs=[pl.BlockSpec((tm, MLA_Q_RANK), lambda i: (i, TAIL_CQ // MLA_Q_RANK)),
                  pl.BlockSpec((tm, MLA_KV_RANK), lambda i: (i, TAIL_CKV // MLA_KV_RANK)),
                  pl.BlockSpec((tm, LANES), lambda i: (i, TAIL_SMALL // LANES)),
                  full(1, MLA_Q_RANK), full(1, MLA_KV_RANK),
                  full(hw, MLA_Q_RANK), full(MLA_KV_RANK, hw), full(MLA_WIDTH, MLA_KV_RANK),
                  tab, tab, tab, tab_t, tab_t],
        out_specs=[pl.BlockSpec((nkv, hw, ATT_T), lambda i: (i, 0, 0)),
                   pl.BlockSpec((tm, hw), lambda i: (i, 0)),
                   pl.BlockSpec((nkv, MLA_HEADS * V_ROWS, ATT_T), lambda i: (i, 0, 0))],
        out_shape=[jax.ShapeDtypeStruct((t // ATT_T, hw, ATT_T), BF16),
                   jax.ShapeDtypeStruct((t, hw), BF16),
                   jax.ShapeDtypeStruct((t // ATT_T, MLA_HEADS * V_ROWS, ATT_T), BF16)],
        compiler_params=_cparams(("parallel",)),
        name="mla_prep",
    )(tail, tail, tail, qn, kvn, wqt, wk, wvt, cos, sa, sb, cost, sint)


ATT_QT = 4


def _attn_kernel(qt_ref, k_ref, vt_ref, o_ref, acc_scr, m_scr, s0, s1, mb0, mb1):
    t = ATT_T
    nq = qt_ref.shape[0]
    npos = 2 * nq + 2
    a = 2 * pl.program_id(2)
    tiles = (a, nq - 1 - a, a + 1, nq - 2 - a)
    m_scr[...] = jnp.full(m_scr.shape, MASK_VALUE, F32)
    acc_scr[...] = jnp.zeros(acc_scr.shape, F32)

    def locate(pos):
        if isinstance(pos, int) and pos < ATT_QT:
            return pos, tiles[pos], tiles[pos]
        r = pos - ATT_QT
        c0, c1, c2 = tiles[0], tiles[0] + tiles[1], tiles[0] + tiles[1] + tiles[2]
        slot = (r >= c0).astype(jnp.int32) + (r >= c1).astype(jnp.int32) + (r >= c2).astype(jnp.int32)
        start = jnp.where(slot == 0, 0, jnp.where(slot == 1, c0, jnp.where(slot == 2, c1, c2)))
        qtile = jnp.where(slot == 0, tiles[0], jnp.where(slot == 1, tiles[1],
                                                         jnp.where(slot == 2, tiles[2], tiles[3])))
        return slot, qtile, r - start

    def scores(pos, s_buf, mb_buf):
        _, qtile, kidx = locate(pos)
        masked = isinstance(pos, int) and pos < ATT_QT
        off = pl.multiple_of(kidx * t, t)
        for hh in range(ATT_HP):
            hsl = slice(hh * HEAD_PAD, (hh + 1) * HEAD_PAD)
            q_t = qt_ref[qtile, hsl, :]
            if not masked:
                s = jnp.dot(k_ref[pl.ds(off, t), hsl], q_t, preferred_element_type=F32)
                s_buf[hh] = s
                mb_buf[hh:hh + 1, :] = jnp.max(s, axis=0, keepdims=True)
                continue
            hf = t // 2
            kv_i = lax.broadcasted_iota(jnp.int32, (hf, t), 0)
            q_i = lax.broadcasted_iota(jnp.int32, (hf, t), 1)
            s_top = jnp.dot(k_ref[pl.ds(off, hf), hsl], q_t, preferred_element_type=F32)
            s_top = jnp.where(kv_i <= q_i, s_top, MASK_VALUE)
            s_low = jnp.dot(k_ref[pl.ds(off + hf, hf), hsl], q_t[:, hf:], preferred_element_type=F32)
            kv_l = lax.broadcasted_iota(jnp.int32, (hf, hf), 0)
            q_l = lax.broadcasted_iota(jnp.int32, (hf, hf), 1)
            s_low = jnp.where(kv_l <= q_l, s_low, MASK_VALUE)
            s_buf[hh, 0:hf, :] = s_top
            s_buf[hh, hf:t, 0:hf] = jnp.full((hf, hf), MASK_VALUE, F32)
            s_buf[hh, hf:t, hf:t] = s_low
            m_left = jnp.max(s_top[:, 0:hf], axis=0, keepdims=True)
            m_right = jnp.maximum(jnp.max(s_top[:, hf:t], axis=0, keepdims=True),
                                  jnp.max(s_low, axis=0, keepdims=True))
            mb_buf[hh:hh + 1, :] = jnp.concatenate([m_left, m_right], axis=1)

    def update(pos, s_buf, mb_buf):
        slot, _, kidx = locate(pos)
        for hh in range(ATT_HP):
            rows = slice(hh * V_ROWS, (hh + 1) * V_ROWS)
            m_old = m_scr[slot, hh:hh + 1, :]
            m_new = jnp.maximum(m_old, mb_buf[hh:hh + 1, :])
            alpha = jnp.exp2(m_old - m_new)
            p = jnp.exp2(s_buf[hh] - m_new).astype(BF16)
            m_scr[slot, hh:hh + 1, :] = m_new
            pv = jnp.dot(vt_ref[kidx, rows, :], p, preferred_element_type=F32)
            acc_scr[slot, rows, :] = alpha * acc_scr[slot, rows, :] + pv

    def pair(i):
        scores(2 * i + 1, s1, mb1)
        update(2 * i, s0, mb0)
        scores(2 * i + 2, s0, mb0)
        update(2 * i + 1, s1, mb1)

    scores(0, s0, mb0)
    for i in range(ATT_QT // 2):
        pair(i)

    first = ATT_QT // 2
    n_pairs = npos // 2 - 1 - first
    unroll = max(u for u in (7, 2, 1) if n_pairs % u == 0)

    def body(j, carry):
        for u in range(unroll):
            pair(first + unroll * j + u)
        return carry

    lax.fori_loop(0, n_pairs // unroll, body, 0)
    scores(npos - 1, s1, mb1)
    update(npos - 2, s0, mb0)
    update(npos - 1, s1, mb1)

    for slot in range(ATT_QT):
        outs = []
        for hh in range(ATT_HP):
            r0 = hh * V_ROWS
            outs.append(acc_scr[slot, r0:r0 + MLA_V, :] / acc_scr[slot, r0 + MLA_V:r0 + MLA_V + 1, :])
        off = pl.multiple_of(tiles[slot] * t, t)
        o_ref[pl.ds(off, t), :] = jnp.concatenate(outs, axis=0).T.astype(BF16)


def _mla_attn(q_t, k, v_t, bsz, seq):
    nq = seq // ATT_T
    nhp = MLA_HEADS // ATT_HP
    assert nq % ATT_QT == 0
    return pl.pallas_call(
        _attn_kernel,
        grid=(bsz, nhp, nq // ATT_QT),
        in_specs=[pl.BlockSpec((nq, ATT_HP * HEAD_PAD, ATT_T), lambda b, h, i: (b, h, 0)),
                  pl.BlockSpec((seq, ATT_HP * HEAD_PAD), lambda b, h, i: (b, h)),
                  pl.BlockSpec((nq, ATT_HP * V_ROWS, ATT_T), lambda b, h, i: (b, h, 0))],
        out_specs=pl.BlockSpec((seq, ATT_HP * MLA_V), lambda b, h, i: (b, h)),
        out_shape=jax.ShapeDtypeStruct((bsz * seq, MLA_WIDTH), BF16),
        scratch_shapes=[pltpu.VMEM((ATT_QT, ATT_HP * V_ROWS, ATT_T), F32),
                        pltpu.VMEM((ATT_QT, 8, ATT_T), F32),
                        pltpu.VMEM((ATT_HP, ATT_T, ATT_T), F32),
                        pltpu.VMEM((ATT_HP, ATT_T, ATT_T), F32),
                        pltpu.VMEM((8, ATT_T), F32),
                        pltpu.VMEM((8, ATT_T), F32)],
        compiler_params=_cparams(("parallel", "parallel", "arbitrary")),
        name="mla_attn",
    )(q_t, k, v_t)


def _merge_kernel(x_ref, ya_ref, yb_ref, o_ref, zc_ref, g0_ref, g1_ref, g2_ref,
                  wa_ref, wb_ref, wc_ref, wo_ref, pn_ref, out_ref):
    dot = lambda a, w: jnp.dot(a, w[...], preferred_element_type=F32)
    sig = lambda r: jax.nn.sigmoid(r[...].astype(F32))
    yc = (o_ref[...].astype(F32) * _silu(zc_ref[...].astype(F32))).astype(BF16)
    merged = (sig(g0_ref) * dot(ya_ref[...], wa_ref)
              + sig(g1_ref) * dot(yb_ref[...], wb_ref)
              + sig(g2_ref) * dot(yc, wc_ref))
    out = dot(merged.astype(BF16), wo_ref)
    out_ref[...] = x_ref[...] + _rms(out, pn_ref[...])


SQ_A, SQ_B, SQ_C, SQ_O, SQ_GLU_T = range(5)


def _merge(x2d, ya, yb, o, main, w_sq, layer, pn, tm):
    t = x2d.shape[0]
    col = lambda j: pl.BlockSpec((tm, 1024), lambda i: (i, j))
    wsel = lambda k: pl.BlockSpec((None, None, 1024, 1024), lambda i: (layer, k, 0, 0))
    return pl.pallas_call(
        _merge_kernel,
        grid=(t // tm,),
        in_specs=[col(0), col(0), col(0), col(0), col(MAIN_ZC), col(MAIN_G0), col(MAIN_G0 + 1),
                  col(MAIN_G0 + 2), wsel(SQ_A), wsel(SQ_B), wsel(SQ_C), wsel(SQ_O),
                  pl.BlockSpec((1, 1024), lambda i: (0, 0))],
        out_specs=col(0),
        out_shape=jax.ShapeDtypeStruct((t, D_MODEL), F32),
        compiler_params=_cparams(("parallel",)),
        name="merge",
    )(x2d, ya, yb, o, main, main, main, main, w_sq, w_sq, w_sq, w_sq, pn)


def _pack_w_in(w):
    sizes = (SSD_D_INNER, SSD_D_INNER + 2 * SSD_GROUPS * SSD_STATE, SSD_HEADS, S5_WIDTH, S5_WIDTH,
             MLA_Q_RANK, MLA_KV_RANK, MLA_ROPE, MLA_WIDTH, 3 * D_MODEL)
    parts, start = [], 0
    for n in sizes:
        parts.append(w[..., start:start + n])
        start += n
    z_a, xbc, dt, u_b, z_b, c_q, c_kv, k_rope, z_c, gates = parts
    zeros = lambda n: jnp.zeros(w.shape[:-1] + (n,), w.dtype)
    small = jnp.concatenate([dt, zeros(ROPE_LO - SSD_HEADS), k_rope,
                             zeros(LANES - ROPE_LO - MLA_ROPE)], axis=-1)
    tail = jnp.concatenate([c_q, c_kv, small, zeros(1024 - TAIL_SMALL - LANES)], axis=-1)
    w_cat = jnp.concatenate([z_a, xbc, z_c, gates, tail], axis=-1).astype(BF16)
    w_t = jnp.swapaxes(jnp.concatenate([u_b, z_b], axis=-1), -1, -2).astype(BF16)
    return w_cat, w_t


def _pad_heads(w, lo, width):
    r = w.shape[0]
    w3 = w.reshape(r, MLA_HEADS, -1)[:, :, lo:lo + width]
    w3 = jnp.pad(w3, ((0, 0), (0, 0), (0, HEAD_PAD - width)))
    return w3.reshape(r, MLA_HEADS * HEAD_PAD)


def _row(v, n=None):
    v = v.reshape(1, -1).astype(F32)
    if n is not None and v.shape[1] < n:
        v = jnp.pad(v, ((0, 0), (0, n - v.shape[1])))
    return v


def _dup(a, axis):
    return jnp.concatenate([a, a], axis=axis).astype(F32)


def kernel(x, pre_norm, w_in, conv_w, conv_b, dt_bias, a_log, d_ssd, ssd_norm, w_a, s5_log_step, s5_lambda_re, s5_lambda_im, s5_b_re, s5_b_im, s5_c_re, s5_c_im, s5_d, w_glu, b_glu, w_b, q_norm, w_uq, kv_norm, w_ukv, w_c, w_o, post_norm):
    bsz, seq, _ = x.shape
    depth = w_in.shape[0]
    t = bsz * seq
    assert seq % ATT_T == 0 and seq % SSD_CHUNK == 0 and seq % S5_L == 0
    tm = min(1024, seq)

    half = MLA_ROPE // 2
    inv_freq = ROPE_THETA ** (-jnp.arange(0, MLA_ROPE, 2, dtype=F32) / MLA_ROPE)
    freq_lane = jnp.concatenate([jnp.zeros((ROPE_LO,), F32), inv_freq, inv_freq,
                                 jnp.zeros((HEAD_PAD - ROPE_LO - 2 * half,), F32)]).reshape(1, HEAD_PAD)
    tables = _rope_tables(freq_lane, inv_freq.reshape(half, 1), seq, min(512, seq))

    w_cat, w_t = _pack_w_in(w_in)
    w_sq = jnp.stack([w_a, w_b, w_c, w_o, jnp.swapaxes(w_glu, 1, 2)], axis=1).astype(BF16)

    x2d = x.reshape(t, D_MODEL)
    for l in range(depth):
        main, tail, ubt = _inproj(x2d, _row(pre_norm[l]), w_cat, w_t, l, tm)

        dsk = jnp.repeat(d_ssd[l].astype(F32), SSD_HEAD_DIM).reshape(1, SSD_D_INNER)
        y_a = _ssd(main, tail, conv_w[l].astype(BF16), _row(conv_b[l]), _row(dt_bias[l], LANES),
                   _row(a_log[l], LANES), dsk, _row(ssd_norm[l]), bsz, seq)

        ls3 = s5_log_step[l].astype(F32).reshape(S5_GROUPS, 1, 1)
        lam = jnp.stack([s5_lambda_re[l], s5_lambda_im[l]], axis=1)
        lrow = _dup(lam, 2)
        lcol = jnp.swapaxes(lrow, 1, 2)
        bt = _dup(jnp.stack([jnp.swapaxes(s5_b_re[l], 1, 2),
                             jnp.swapaxes(s5_b_im[l], 1, 2)], axis=1), 3)
        cn = _dup(jnp.stack([s5_c_re[l], s5_c_im[l]], axis=1), 3)
        ct = jnp.swapaxes(cn, 2, 3)
        y_t = _s5(ubt, ls3, lrow, lcol, bt, cn, ct, bsz, seq)
        y_b = _s5_post(y_t, ubt, s5_d[l].astype(F32).reshape(S5_WIDTH, 1), w_sq, l,
                       b_glu[l].astype(F32).reshape(S5_WIDTH, 1), min(512, seq))

        stride_q = MLA_NOPE + MLA_ROPE
        wqt = _pad_heads(w_uq[l], 0, stride_q).T.astype(BF16)
        wk = _pad_heads(w_ukv[l], 0, MLA_NOPE).astype(BF16)
        wvt = (w_ukv[l].reshape(MLA_KV_RANK, MLA_HEADS, MLA_NOPE + MLA_V)[:, :, MLA_NOPE:]
               .reshape(MLA_KV_RANK, MLA_WIDTH).T.astype(BF16))
        q_t, k, v_t = _mla_prep(tail, _row(q_norm[l]), _row(kv_norm[l]), wqt, wk, wvt,
                                tables, seq, tm)
        o = _mla_attn(q_t, k, v_t, bsz, seq)

        x2d = _merge(x2d, y_a, y_b, o, main, w_sq, l, _row(post_norm[l]), min(512, seq))
    return x2d.reshape(bsz, seq, D_MODEL)
```
